```python
import jax, jax.numpy as jnp
from jax import lax
import numpy as np

D_MODEL = 2048
BATCH = 2
SEQ = 4096
DEPTH = 1
DEC_BATCH = 128
DEC_SEQ = 8
PAST_LEN = 16384
PAGE_SIZE = 128

MLA_HEADS = 8
MLA_NOPE = 128
MLA_ROPE = 64
MLA_QK = MLA_NOPE + MLA_ROPE
MLA_V = 128
MLA_Q_RANK = 512
MLA_KV_RANK = 256
MLA_WIDTH = MLA_HEADS * MLA_V
MLA_SCALE = MLA_QK ** -0.5
ROPE_THETA = 10000.0
Q_BLOCK = 128
GLA_HEADS = 4
GLA_DK = 128
GLA_DV = 256
GLA_KEY = GLA_HEADS * GLA_DK
GLA_WIDTH = GLA_HEADS * GLA_DV
GLA_GATE_RANK = 16
GLA_TAU = 16.0
GLA_CHUNK = 64
MIX_WIDTH = MLA_WIDTH + GLA_WIDTH
IN_SPLITS = (MLA_Q_RANK, MLA_KV_RANK, MLA_ROPE, GLA_KEY, GLA_KEY, GLA_WIDTH, GLA_WIDTH, GLA_GATE_RANK)
IN_COLS = sum(IN_SPLITS)
PEER_HEADS = 8
PEER_N_KEYS = 128
PEER_EXPERTS = PEER_N_KEYS * PEER_N_KEYS
PEER_DKEY = 256
PEER_TOPK = 16
PEER_BLOCK = 64
PLE_DIM = 256
EPS = 1e-6

kernel_name = "hymba_mla_gla_peer_decode_step"

F32 = jnp.float32


def rmsnorm(x, g):
    x32 = x.astype(F32)
    y = x32 * lax.rsqrt(jnp.mean(jnp.square(x32), -1, keepdims=True) + EPS)
    return (y * g.astype(F32)).astype(x.dtype)


def rope(x, pos):
    half = x.shape[-1] // 2
    inv_freq = ROPE_THETA ** (-jnp.arange(half, dtype=F32) / half)
    ang = pos.astype(F32)[:, None] * inv_freq
    ang = ang.reshape((1, pos.shape[0]) + (1,) * (x.ndim - 3) + (half,))
    cos, sin = jnp.cos(ang), jnp.sin(ang)
    x32 = x.astype(F32)
    x1, x2 = x32[..., :half], x32[..., half:]
    return jnp.concatenate([x1 * cos - x2 * sin, x2 * cos + x1 * sin], -1).astype(x.dtype)


def mixer_inputs(x, w):
    xn = rmsnorm(x, w["norm_mix"])
    z = xn @ w["w_in"]
    cuts = [int(c) for c in np.cumsum(IN_SPLITS)[:-1]]
    return jnp.split(z, cuts, axis=-1)


def mla_qkv(cq_raw, ckv_raw, kr_raw, pos, w):
    B, T = cq_raw.shape[:2]
    cq = rmsnorm(cq_raw, w["norm_q_lat"])
    q = rmsnorm((cq @ w["w_uq"]).reshape(B, T, MLA_HEADS, MLA_QK), w["qk_gain_q"])
    q_nope, q_rope = q[..., :MLA_NOPE], rope(q[..., MLA_NOPE:], pos)
    c = rmsnorm(ckv_raw, w["norm_kv_lat"])
    k_nope = jnp.einsum("btr,rhd->bthd", c, w["w_uk"])
    ss = jnp.sum(jnp.square(k_nope.astype(F32)), -1) + jnp.sum(jnp.square(kr_raw.astype(F32)), -1)[..., None]
    inv_rms = lax.rsqrt(ss / MLA_QK + EPS).astype(c.dtype)
    kr = rope(kr_raw * w["qk_gain_k"][MLA_NOPE:], pos)
    return q_nope, q_rope, c, k_nope, kr, inv_rms


def mla_prompt(q_nope, q_rope, c, k_nope, kr, inv_rms, w):
    B, S = c.shape[:2]
    k = jnp.concatenate([k_nope * w["qk_gain_k"][:MLA_NOPE],
                         jnp.broadcast_to(kr[:, :, None], (B, S, MLA_HEADS, MLA_ROPE))], -1) * inv_rms[..., None]
    q = jnp.concatenate([q_nope, q_rope], -1)
    v = jnp.einsum("btr,rhd->bthd", c, w["w_uv"])
    nb = S // Q_BLOCK
    qb = jnp.moveaxis(q.reshape(B, nb, Q_BLOCK, MLA_HEADS, MLA_QK), 1, 0)
    key_pos = jnp.arange(S)

    def block(args):
        qi, bi = args
        s = jnp.einsum("bqhd,bshd->bhqs", qi, k).astype(F32) * MLA_SCALE
        qpos = bi * Q_BLOCK + jnp.arange(Q_BLOCK)
        s = jnp.where(key_pos[None, :] <= qpos[:, None], s, -jnp.inf)
        p = jax.nn.softmax(s, -1).astype(v.dtype)
        return jnp.einsum("bhqs,bshd->bqhd", p, v)

    o = lax.map(block, (qb, jnp.arange(nb)))
    return jnp.moveaxis(o, 0, 1).reshape(B, S, MLA_WIDTH)


def mla_sample(q_nope, q_rope, c, kr, inv_rms, lat_pool, kr_pool, ir_pool, page_table, w):
    B, T = c.shape[:2]
    past = page_table.shape[1] * lat_pool.shape[1]
    lat_p = lat_pool[page_table].reshape(B, past, MLA_KV_RANK)
    kr_p = kr_pool[page_table].reshape(B, past, MLA_ROPE)
    ir_p = ir_pool[page_table].reshape(B, past, MLA_HEADS)
    q_lat = jnp.einsum("bthd,rhd->bthr", q_nope * w["qk_gain_k"][:MLA_NOPE], w["w_uk"])

    def scores(lat, krr, ir):
        s = jnp.einsum("bthr,bsr->bhts", q_lat, lat) + jnp.einsum("bthd,bsd->bhts", q_rope, krr)
        return s.astype(F32) * jnp.transpose(ir, (0, 2, 1))[:, :, None, :].astype(F32) * MLA_SCALE

    causal = jnp.tril(jnp.ones((T, T), bool))
    s_all = jnp.concatenate([scores(lat_p, kr_p, ir_p), jnp.where(causal, scores(c, kr, inv_rms), -jnp.inf)], -1)
    p = jax.nn.softmax(s_all, -1).astype(c.dtype)
    o_lat = jnp.einsum("bhts,bsr->bthr", p[..., :past], lat_p) + jnp.einsum("bhts,bsr->bthr", p[..., past:], c)
    o = jnp.einsum("bthr,rhd->bthd", o_lat, w["w_uv"])
    return o.reshape(B, T, MLA_WIDTH)


def gla_inputs(qg, kg, vg, a_lr, w):
    B, T = qg.shape[:2]
    q = qg.reshape(B, T, GLA_HEADS, GLA_DK).astype(F32) * GLA_DK ** -0.5
    k = kg.reshape(B, T, GLA_HEADS, GLA_DK).astype(F32)
    v = vg.reshape(B, T, GLA_HEADS, GLA_DV).astype(F32)
    g = jax.nn.log_sigmoid((a_lr @ w["w_gla_a"] + w["b_gla_a"]).astype(F32)) / GLA_TAU
    return q, k, v, g.reshape(B, T, GLA_HEADS, GLA_DK)


def gla_chunk(S0, q, k, v, g):
    C = q.shape[1]
    b = jnp.cumsum(g, axis=1)
    b_last = b[:, -1]
    o_inter = jnp.einsum("bthk,bhkv->bthv", q * jnp.exp(b), S0)
    causal = jnp.tril(jnp.ones((C, C), bool))[None, :, :, None, None]
    decay = jnp.exp(jnp.where(causal, b[:, :, None] - b[:, None], -jnp.inf))
    A = jnp.einsum("bthk,bshk,btshk->bhts", q, k, decay)
    o = o_inter + jnp.einsum("bhts,bshv->bthv", A, v)
    S = jnp.exp(b_last)[..., None] * S0 + jnp.einsum("bshk,bshv->bhkv", k * jnp.exp(b_last[:, None] - b), v)
    return S, o


def gla_prompt(q, k, v, g):
    B, S = q.shape[:2]
    nc = S // GLA_CHUNK

    def to_chunks(a):
        return jnp.moveaxis(a.reshape((B, nc, GLA_CHUNK) + a.shape[2:]), 1, 0)

    S0 = jnp.zeros((B, GLA_HEADS, GLA_DK, GLA_DV), F32)
    S_fin, o = lax.scan(lambda s, xs: gla_chunk(s, *xs), S0, (to_chunks(q), to_chunks(k), to_chunks(v), to_chunks(g)))
    return jnp.moveaxis(o, 0, 1).reshape(B, S, GLA_HEADS, GLA_DV), S_fin


def gla_output(o, r, w, dtype):
    B, T = o.shape[:2]
    y = rmsnorm(o, w["norm_gla_out"]) * jax.nn.silu(r.reshape(B, T, GLA_HEADS, GLA_DV).astype(F32))
    return y.reshape(B, T, GLA_WIDTH).astype(dtype)


def peer(xn, w):
    B, T, D = xn.shape
    n = B * T
    nb = -(-n // PEER_BLOCK)
    xf = jnp.pad(xn.reshape(n, D), ((0, nb * PEER_BLOCK - n), (0, 0))).reshape(nb, PEER_BLOCK, D)

    def block(xb):
        q = (xb @ w["peer_w_q"]).reshape(PEER_BLOCK, PEER_HEADS, 2, PEER_DKEY // 2)
        s1 = jnp.einsum("thd,kd->thk", q[:, :, 0], w["peer_keys1"])
        s2 = jnp.einsum("thd,kd->thk", q[:, :, 1], w["peer_keys2"])
        v1, i1 = lax.top_k(s1, PEER_TOPK)
        v2, i2 = lax.top_k(s2, PEER_TOPK)
        cand = (v1[..., :, None] + v2[..., None, :]).reshape(PEER_BLOCK, PEER_HEADS, PEER_TOPK * PEER_TOPK)
        cidx = (i1[..., :, None] * PEER_N_KEYS + i2[..., None, :]).reshape(PEER_BLOCK, PEER_HEADS, PEER_TOPK * PEER_TOPK)
        top_s, sel = lax.top_k(cand, PEER_TOPK)
        e = jnp.take_along_axis(cidx, sel, -1)
        gate = jax.nn.softmax(top_s.astype(F32), -1)
        act = jax.nn.gelu(jnp.einsum("td,thkd->thk", xb, w["peer_u"][e]).astype(F32), approximate=False)
        coef = (gate * act).astype(xb.dtype)
        return jnp.einsum("thk,thkd->td", coef, w["peer_v"][e])

    y = lax.map(block, xf)
    return y.reshape(nb * PEER_BLOCK, D)[:n].reshape(B, T, D)


def finish_layer(x, mla_o, gla_o, p_i, w):
    mix = jnp.concatenate([rmsnorm(mla_o, w["norm_mla_out"]), gla_o], -1) @ w["w_out"]
    h = x + mix
    h = h + peer(rmsnorm(h, w["norm_ffn"]), w)
    gate = jax.nn.sigmoid((rmsnorm(h, w["norm_ple"]) @ w["w_ple_gate"]).astype(F32)).astype(h.dtype)
    return h + gate * (p_i @ w["w_ple_proj"])


def setup_inputs(seed: int = 0) -> dict:
    key = jax.random.key(seed)
    ks = jax.random.split(key, 40)
    n_pages = PAST_LEN // PAGE_SIZE
    n_used = DEC_BATCH * n_pages
    n_pool = n_used + max(1, n_used // 4)

    def nrm(i, shape, scale):
        return scale * jax.random.normal(ks[i], shape, F32)

    def gain(i, shape):
        return 1.0 + 0.02 * jax.random.normal(ks[i], shape, F32)

    page_table = jax.random.permutation(ks[39], n_pool)[:n_used].reshape(DEC_BATCH, n_pages).astype(jnp.int32)
    return {
        "x_prompt": nrm(0, (BATCH, SEQ, D_MODEL), 1.0),
        "x_sample": nrm(1, (DEC_BATCH, DEC_SEQ, D_MODEL), 1.0),
        "cache_kv_latent": nrm(2, (DEPTH, n_pool, PAGE_SIZE, MLA_KV_RANK), 1.0),
        "cache_k_rope": nrm(3, (DEPTH, n_pool, PAGE_SIZE, MLA_ROPE), 1.0),
        "cache_k_inv_rms": jax.random.uniform(ks[4], (DEPTH, n_pool, PAGE_SIZE, MLA_HEADS), F32, 0.5, 1.5),
        "state_gla": nrm(5, (DEPTH, DEC_BATCH, GLA_HEADS, GLA_DK, GLA_DV), 0.5),
        "page_table": page_table,
        "p_prompt": nrm(6, (DEPTH, BATCH, SEQ, PLE_DIM), 1.0),
        "p_sample": nrm(7, (DEPTH, DEC_BATCH, DEC_SEQ, PLE_DIM), 1.0),
        "norm_mix": gain(8, (DEPTH, D_MODEL)),
        "w_in": nrm(9, (DEPTH, D_MODEL, IN_COLS), D_MODEL ** -0.5),
        "norm_q_lat": gain(10, (DEPTH, MLA_Q_RANK)),
        "w_uq": nrm(11, (DEPTH, MLA_Q_RANK, MLA_HEADS * MLA_QK), MLA_Q_RANK ** -0.5),
        "norm_kv_lat": gain(12, (DEPTH, MLA_KV_RANK)),
        "w_uk": nrm(13, (DEPTH, MLA_KV_RANK, MLA_HEADS, MLA_NOPE), MLA_KV_RANK ** -0.5),
        "w_uv": nrm(14, (DEPTH, MLA_KV_RANK, MLA_HEADS, MLA_V), MLA_KV_RANK ** -0.5),
        "qk_gain_q": gain(15, (DEPTH, MLA_QK)),
        "qk_gain_k": gain(16, (DEPTH, MLA_QK)),
        "norm_mla_out": gain(17, (DEPTH, MLA_WIDTH)),
        "w_gla_a": nrm(18, (DEPTH, GLA_GATE_RANK, GLA_KEY), GLA_GATE_RANK ** -0.5),
        "b_gla_a": nrm(19, (DEPTH, GLA_KEY), 0.1),
        "norm_gla_out": gain(20, (DEPTH, GLA_DV)),
        "w_out": nrm(21, (DEPTH, MIX_WIDTH, D_MODEL), MIX_WIDTH ** -0.5),
        "norm_ffn": gain(22, (DEPTH, D_MODEL)),
        "peer_w_q": nrm(23, (DEPTH, D_MODEL, PEER_HEADS * PEER_DKEY), D_MODEL ** -0.5),
        "peer_keys1": nrm(24, (DEPTH, PEER_N_KEYS, PEER_DKEY // 2), (PEER_DKEY // 2) ** -0.5),
        "peer_keys2": nrm(25, (DEPTH, PEER_N_KEYS, PEER_DKEY // 2), (PEER_DKEY // 2) ** -0.5),
        "peer_u": nrm(26, (DEPTH, PEER_EXPERTS, D_MODEL), D_MODEL ** -0.5),
        "peer_v": nrm(27, (DEPTH, PEER_EXPERTS, D_MODEL), 0.3),
        "norm_ple": gain(28, (DEPTH, D_MODEL)),
        "w_ple_gate": nrm(29, (DEPTH, D_MODEL, D_MODEL), D_MODEL ** -0.5),
        "w_ple_proj": nrm(30, (DEPTH, PLE_DIM, D_MODEL), PLE_DIM ** -0.5),
    }


def reference(x_prompt, x_sample, cache_kv_latent, cache_k_rope, cache_k_inv_rms, state_gla, page_table,
              p_prompt, p_sample, norm_mix, w_in, norm_q_lat, w_uq, norm_kv_lat, w_uk, w_uv, qk_gain_q, qk_gain_k,
              norm_mla_out, w_gla_a, b_gla_a, norm_gla_out, w_out, norm_ffn, peer_w_q, peer_keys1, peer_keys2,
              peer_u, peer_v, norm_ple, w_ple_gate, w_ple_proj):
    S = x_prompt.shape[1]
    T = x_sample.shape[1]
    past_len = page_table.shape[1] * cache_kv_latent.shape[2]
    pos_p = jnp.arange(S, dtype=F32)
    pos_s = past_len + jnp.arange(T, dtype=F32)
    hp, hs = x_prompt, x_sample
    lat_p, kr_p, ir_p, gla_p = [], [], [], []
    lat_s, kr_s, ir_s, gla_s = [], [], [], []
    for i in range(DEPTH):
        w = {"norm_mix": norm_mix[i], "w_in": w_in[i], "norm_q_lat": norm_q_lat[i], "w_uq": w_uq[i],
             "norm_kv_lat": norm_kv_lat[i], "w_uk": w_uk[i], "w_uv": w_uv[i], "qk_gain_q": qk_gain_q[i],
             "qk_gain_k": qk_gain_k[i], "norm_mla_out": norm_mla_out[i], "w_gla_a": w_gla_a[i],
             "b_gla_a": b_gla_a[i], "norm_gla_out": norm_gla_out[i], "w_out": w_out[i], "norm_ffn": norm_ffn[i],
             "peer_w_q": peer_w_q[i], "peer_keys1": peer_keys1[i], "peer_keys2": peer_keys2[i],
             "peer_u": peer_u[i], "peer_v": peer_v[i], "norm_ple": norm_ple[i], "w_ple_gate": w_ple_gate[i],
             "w_ple_proj": w_ple_proj[i]}
        cq, ckv, kr_raw, qg, kg, vg, r, a_lr = mixer_inputs(hp, w)
        q_nope, q_rope, c, k_nope, kr, inv_rms = mla_qkv(cq, ckv, kr_raw, pos_p, w)
        mla_o = mla_prompt(q_nope, q_rope, c, k_nope, kr, inv_rms, w)
        gq, gk, gv, gg = gla_inputs(qg, kg, vg, a_lr, w)
        o, s_fin = gla_prompt(gq, gk, gv, gg)
        hp_next = finish_layer(hp, mla_o, gla_output(o, r, w, hp.dtype), p_prompt[i], w)
        lat_p.append(c)
        kr_p.append(kr)
        ir_p.append(inv_rms)
        gla_p.append(s_fin)
        cq, ckv, kr_raw, qg, kg, vg, r, a_lr = mixer_inputs(hs, w)
        q_nope, q_rope, c, k_nope, kr, inv_rms = mla_qkv(cq, ckv, kr_raw, pos_s, w)
        mla_o = mla_sample(q_nope, q_rope, c, kr, inv_rms, cache_kv_latent[i], cache_k_rope[i],
                           cache_k_inv_rms[i], page_table, w)
        gq, gk, gv, gg = gla_inputs(qg, kg, vg, a_lr, w)
        s_new, o = gla_chunk(state_gla[i].astype(F32), gq, gk, gv, gg)
        hs_next = finish_layer(hs, mla_o, gla_output(o, r, w, hs.dtype), p_sample[i], w)
        lat_s.append(c)
        kr_s.append(kr)
        ir_s.append(inv_rms)
        gla_s.append(s_new)
        hp, hs = hp_next, hs_next
    new_lat_p = jnp.stack(lat_p, 0)
    new_kr_p = jnp.stack(kr_p, 0)
    new_ir_p = jnp.stack(ir_p, 0)
    new_gla_p = jnp.stack(gla_p, 0)
    new_lat_s = jnp.stack(lat_s, 0)
    new_kr_s = jnp.stack(kr_s, 0)
    new_ir_s = jnp.stack(ir_s, 0)
    new_gla_s = jnp.stack(gla_s, 0)
    return (hp, hs, new_lat_p, new_kr_p, new_ir_p, new_gla_p, new_lat_s, new_kr_s, new_ir_s, new_gla_s)
```

```python
import functools

import jax
import jax.numpy as jnp
from jax import lax
from jax.experimental import pallas as pl
from jax.experimental.pallas import tpu as pltpu

F32 = jnp.float32
BF16 = jnp.bfloat16
EPS = 1e-6
NEG_INF = float("-inf")
POS_INF = float("inf")

MLA_HEADS = 8
MLA_NOPE = 128
MLA_ROPE = 64
MLA_QK = MLA_NOPE + MLA_ROPE
MLA_V = 128
MLA_Q_RANK = 512
MLA_KV_RANK = 256
MLA_SCALE = MLA_QK ** -0.5
MLA_HEAD_PAD = 256
ROPE_THETA = 10000.0
GLA_HEADS = 4
GLA_DK = 128
GLA_DV = 256
GLA_GATE_RANK = 16
GLA_TAU = 16.0
GLA_CHUNK = 64
GLA_SUB = 16
PEER_HEADS = 8
PEER_N_KEYS = 128
PEER_TOPK = 16
PLE_DIM = 256

Z_COLS = 4096
ZB_CQ = (512, 0)
ZB_GQ = (512, 1)
ZB_GK = (512, 2)
ZB_CKV = (256, 6)
ZB_KR = (128, 14)
ZB_ALR = (128, 15)
ZB_GV = (1024, 2)
ZB_R = (1024, 3)

V7X_VMEM_LIMIT = 56 * 1024 * 1024

TM_IN, TN_IN = 512, 1024
TM_PREP = 256
FLASH_TILE = 512
DECODE_PAGES = 16
GLA_SAMPLE_ROWS = 8
TM_OUT = 256
TT_ROUTE = 256
TT_DENSE, TE_DENSE = 512, 1024
TM_PLE = 256


def _cparams(*sem):
    return pltpu.CompilerParams(dimension_semantics=sem, vmem_limit_bytes=V7X_VMEM_LIMIT)


def _dot(a, b, **kw):
    return jnp.dot(a, b, preferred_element_type=F32, **kw)


def _dot_nt(a, b):
    return lax.dot_general(a, b, (((1,), (1,)), ((), ())), preferred_element_type=F32)


def _dot_tn(a, b, **kw):
    return lax.dot_general(a, b, (((0,), (0,)), ((), ())), preferred_element_type=F32, **kw)


def _rms(x, g):
    return x * lax.rsqrt(jnp.mean(x * x, -1, keepdims=True) + EPS) * g


def _rot_cols(w):
    h = w.shape[-1] // 2
    return jnp.concatenate([-w[..., h:], w[..., :h]], -1)


def _rot_gain(g):
    h = g.shape[-1] // 2
    return jnp.concatenate([g[..., h:], g[..., :h]], -1)


def _norm_matmul_body(x_ref, g_ref, w_ref, o_ref, xn_sc):
    @pl.when(pl.program_id(1) == 0)
    def _():
        xn_sc[...] = _rms(x_ref[...], g_ref[...]).astype(BF16)

    o_ref[...] = _dot(xn_sc[...], w_ref[...])


def _norm_matmul(x, g, w, tm, tn):
    n, d = x.shape
    nc = w.shape[1]
    return pl.pallas_call(
        _norm_matmul_body,
        grid=(n // tm, nc // tn),
        in_specs=[pl.BlockSpec((tm, d), lambda i, j: (i, 0)),
                  pl.BlockSpec((1, d), lambda i, j: (0, 0)),
                  pl.BlockSpec((d, tn), lambda i, j: (0, j))],
        out_specs=pl.BlockSpec((tm, tn), lambda i, j: (i, j)),
        out_shape=jax.ShapeDtypeStruct((n, nc), F32),
        scratch_shapes=[pltpu.VMEM((tm, d), BF16)],
        compiler_params=_cparams("parallel", "arbitrary"),
        name="in_proj",
    )(x, g.reshape(1, d), w)


def _mla_prep_body(cq_ref, ckv_ref, krb_ref, tab_ref, nq_ref, nkv_ref, gq_ref, gk_ref, wuq_ref, wuk_ref, wx_ref,
                   *outs, sample):
    if sample:
        ql_ref, qr_ref, lat_ref, kr_ref, ir_ref = outs
    else:
        qt_ref, kt_ref, v_ref, lat_ref, kr_ref, ir_ref = outs
    tab = tab_ref[...]
    gq = gq_ref[...]
    gk = gk_ref[...]
    cqn = _rms(cq_ref[...], nq_ref[...]).astype(BF16)
    qraw = _dot(cqn, wuq_ref[...])
    c = _rms(ckv_ref[...], nkv_ref[...])
    lat_ref[...] = c
    cb = c.astype(BF16)
    kn = _dot(cb, wuk_ref[...])
    if not sample:
        v_ref[...] = _dot(cb, wx_ref[...]).astype(BF16)
    krb = krb_ref[...]
    lane = lax.broadcasted_iota(jnp.int32, krb.shape, 1)
    ss_kr = jnp.sum(jnp.where(lane < MLA_ROPE, krb * krb, 0.0), -1, keepdims=True)
    t = krb * gk[:, MLA_NOPE:] * tab
    kr2 = t + pltpu.roll(t, MLA_ROPE, 1)
    kr_ref[...] = kr2[:, :MLA_ROPE]
    krz = jnp.where(lane < MLA_ROPE, kr2, 0.0)
    lane_q = lax.broadcasted_iota(jnp.int32, (krb.shape[0], MLA_HEAD_PAD), 1)
    ir_acc = jnp.zeros(krb.shape, F32)
    for h in range(MLA_HEADS):
        knh = kn[:, h * MLA_NOPE:(h + 1) * MLA_NOPE]
        ss = jnp.sum(knh * knh, -1, keepdims=True) + ss_kr
        inv = lax.rsqrt(ss / MLA_QK + EPS)
        ir_acc = jnp.where(lane == h, inv, ir_acc)
        qh = qraw[:, h * MLA_HEAD_PAD:(h + 1) * MLA_HEAD_PAD]
        ssq = jnp.sum(jnp.where(lane_q < MLA_QK, qh * qh, 0.0), -1, keepdims=True)
        qinv = lax.rsqrt(ssq / MLA_QK + EPS)
        qn = qh[:, :MLA_NOPE] * qinv * gq[:, :MLA_NOPE]
        tq = qh[:, MLA_NOPE:] * qinv * gq[:, MLA_NOPE:] * tab
        qr2 = tq + pltpu.roll(tq, MLA_ROPE, 1)
        if sample:
            qg = (qn * gk[:, :MLA_NOPE]).astype(BF16)
            ql_ref[h] = _dot(qg, wx_ref[h]) * MLA_SCALE
            qr_ref[h] = qr2[:, :MLA_ROPE] * MLA_SCALE
        else:
            lo = h * MLA_HEAD_PAD
            qt_ref[:, lo:lo + MLA_NOPE] = (qn * MLA_SCALE).astype(BF16)
            qt_ref[:, lo + MLA_NOPE:lo + MLA_HEAD_PAD] = (qr2 * MLA_SCALE).astype(BF16)
            kt_ref[:, lo:lo + MLA_NOPE] = (knh * gk[:, :MLA_NOPE] * inv).astype(BF16)
            kt_ref[:, lo + MLA_NOPE:lo + MLA_HEAD_PAD] = (krz * inv).astype(BF16)
    ir_ref[...] = ir_acc


def _mla_prep(z, tab, tab_blocks, nq, nkv, gq_ext, gk_ext, wuq_ext, wuk, wx, tm, sample):
    n = z.shape[0]
    hp = MLA_HEADS * MLA_HEAD_PAD

    def const(shape):
        return pl.BlockSpec(shape, lambda i: (0,) * len(shape))

    in_specs = [pl.BlockSpec((tm, ZB_CQ[0]), lambda i: (i, ZB_CQ[1])),
                pl.BlockSpec((tm, ZB_CKV[0]), lambda i: (i, ZB_CKV[1])),
                pl.BlockSpec((tm, ZB_KR[0]), lambda i: (i, ZB_KR[1])),
                pl.BlockSpec((tm, 128), lambda i: (i % tab_blocks, 0)),
                const((1, MLA_Q_RANK)), const((1, MLA_KV_RANK)), const((1, MLA_HEAD_PAD)), const((1, MLA_HEAD_PAD)),
                const(wuq_ext.shape), const(wuk.shape), const(wx.shape)]
    row = lambda w: pl.BlockSpec((tm, w), lambda i: (i, 0))
    tail_specs = [row(MLA_KV_RANK), row(MLA_ROPE), row(128)]
    tail_shapes = [jax.ShapeDtypeStruct((n, MLA_KV_RANK), F32), jax.ShapeDtypeStruct((n, MLA_ROPE), F32),
                   jax.ShapeDtypeStruct((n, 128), F32)]
    if sample:
        head = lambda w: pl.BlockSpec((MLA_HEADS, tm, w), lambda i: (0, i, 0))
        out_specs = [head(MLA_KV_RANK), head(MLA_ROPE)] + tail_specs
        out_shape = [jax.ShapeDtypeStruct((MLA_HEADS, n, MLA_KV_RANK), F32),
                     jax.ShapeDtypeStruct((MLA_HEADS, n, MLA_ROPE), F32)] + tail_shapes
    else:
        out_specs = [row(hp), row(hp), row(MLA_HEADS * MLA_V)] + tail_specs
        out_shape = [jax.ShapeDtypeStruct((n, hp), BF16), jax.ShapeDtypeStruct((n, hp), BF16),
                     jax.ShapeDtypeStruct((n, MLA_HEADS * MLA_V), BF16)] + tail_shapes
    return pl.pallas_call(
        functools.partial(_mla_prep_body, sample=sample),
        grid=(n // tm,),
        in_specs=in_specs, out_specs=out_specs, out_shape=out_shape,
        compiler_params=_cparams("parallel"),
        name="mla_prep_sample" if sample else "mla_prep_prompt",
    )(z, z, z, tab, nq.reshape(1, -1), nkv.reshape(1, -1), gq_ext, gk_ext, wuq_ext, wuk, wx)


def _flash_body(q_ref, k_ref, v_ref, o_ref, m_sc, l_sc, acc_sc, *, tq, tk):
    i = pl.program_id(2)
    j = pl.program_id(3)

    @pl.when(j == 0)
    def _():
        m_sc[...] = jnp.full(m_sc.shape, NEG_INF, F32)
        l_sc[...] = jnp.zeros(l_sc.shape, F32)
        acc_sc[...] = jnp.zeros(acc_sc.shape, F32)

    @pl.when(j * tk <= i * tq + tq - 1)
    def _():
        s = _dot_nt(q_ref[...], k_ref[...])
        qpos = i * tq + lax.broadcasted_iota(jnp.int32, s.shape, 0)
        kpos = j * tk + lax.broadcasted_iota(jnp.int32, s.shape, 1)
        s = jnp.where(kpos <= qpos, s, NEG_INF)
        m_prev = m_sc[...]
        m_new = jnp.maximum(m_prev, jnp.max(s, -1, keepdims=True))
        alpha = jnp.exp(m_prev - m_new)
        p = jnp.exp(s - m_new)
        l_sc[...] = alpha * l_sc[...] + jnp.sum(p, -1, keepdims=True)
        acc_sc[...] = alpha * acc_sc[...] + _dot(p.astype(BF16), v_ref[...])
        m_sc[...] = m_new

    @pl.when(j == pl.num_programs(3) - 1)
    def _():
        o_ref[...] = acc_sc[...] / l_sc[...]


def _flash(qt, kt, v, tq, tk):
    b, s, _ = qt.shape
    last = lambda i: (i * tq + tq - 1) // tk
    return pl.pallas_call(
        functools.partial(_flash_body, tq=tq, tk=tk),
        grid=(b, MLA_HEADS, s // tq, s // tk),
        in_specs=[pl.BlockSpec((None, tq, MLA_HEAD_PAD), lambda bb, h, i, j: (bb, i, h)),
                  pl.BlockSpec((None, tk, MLA_HEAD_PAD), lambda bb, h, i, j: (bb, jnp.minimum(j, last(i)), h)),
                  pl.BlockSpec((None, tk, MLA_V), lambda bb, h, i, j: (bb, jnp.minimum(j, last(i)), h))],
        out_specs=pl.BlockSpec((None, tq, MLA_V), lambda bb, h, i, j: (bb, i, h)),
        out_shape=jax.ShapeDtypeStruct((b, s, MLA_HEADS * MLA_V), F32),
        scratch_shapes=[pltpu.VMEM((tq, 1), F32), pltpu.VMEM((tq, 1), F32), pltpu.VMEM((tq, MLA_V), F32)],
        compiler_params=_cparams("parallel", "parallel", "parallel", "arbitrary"),
        name="mla_prompt_attn",
    )(qt, kt, v)


def _decode_body(pt_ref, ql_ref, qr_ref, cn_ref, krn_ref, irn_ref, wuv_ref, *rest, pages, t_new):
    lat_refs = rest[:pages]
    kr_refs = rest[pages:2 * pages]
    ir_refs = rest[2 * pages:3 * pages]
    o_ref, m_sc, l_sc, acc_sc = rest[3 * pages:]
    rows = MLA_HEADS * t_new
    c = pl.program_id(1)

    @pl.when(c == 0)
    def _():
        m_sc[...] = jnp.full(m_sc.shape, NEG_INF, F32)
        l_sc[...] = jnp.zeros(l_sc.shape, F32)
        acc_sc[...] = jnp.zeros(acc_sc.shape, F32)

    ql = ql_ref[...].reshape(rows, MLA_KV_RANK).astype(BF16)
    qr = qr_ref[...].reshape(rows, MLA_ROPE).astype(BF16)

    def scores(lat_b, kr_b, ir_t):
        s = _dot_nt(ql, lat_b) + _dot_nt(qr, kr_b)
        return (s.reshape(MLA_HEADS, t_new, -1) * ir_t[:, None, :]).reshape(rows, -1)

    def accumulate(ss, lats):
        m_prev = m_sc[...]
        m_new = functools.reduce(jnp.maximum, [jnp.max(s, -1, keepdims=True) for s in ss], m_prev)
        alpha = jnp.exp(m_prev - m_new)
        l = alpha * l_sc[...]
        acc = alpha * acc_sc[...]
        for s, lb in zip(ss, lats):
            p = jnp.exp(s - m_new)
            l = l + jnp.sum(p, -1, keepdims=True)
            acc = acc + _dot(p.astype(BF16), lb)
        m_sc[...] = m_new
        l_sc[...] = l
        acc_sc[...] = acc

    lats, ss = [], []
    for p in range(pages):
        lb = lat_refs[p][...].astype(BF16)
        lats.append(lb)
        ss.append(scores(lb, kr_refs[p][...].astype(BF16), ir_refs[p][...]))
    accumulate(ss, lats)

    @pl.when(c == pl.num_programs(1) - 1)
    def _():
        pad = 128 - t_new
        lb = jnp.concatenate([cn_ref[...], jnp.zeros((pad, MLA_KV_RANK), F32)], 0).astype(BF16)
        kb = jnp.concatenate([krn_ref[...], jnp.zeros((pad, MLA_ROPE), F32)], 0).astype(BF16)
        s = scores(lb, kb, irn_ref[...])
        tq = lax.rem(lax.broadcasted_iota(jnp.int32, s.shape, 0), t_new)
        key = lax.broadcasted_iota(jnp.int32, s.shape, 1)
        accumulate([jnp.where(key <= tq, s, NEG_INF)], [lb])
        o_lat = acc_sc[...] / l_sc[...]
        for h in range(MLA_HEADS):
            o_ref[:, h * MLA_V:(h + 1) * MLA_V] = _dot(o_lat[h * t_new:(h + 1) * t_new].astype(BF16), wuv_ref[h])


def _decode(page_table, ql, qr, c_new, kr_new, ir_new_t, wuv_h, lat_pool, kr_pool, irt_pool, pages):
    bs, n_pages = page_table.shape
    t_new = c_new.shape[1]
    page = lat_pool.shape[1]

    def pool_spec(width_dims, p):
        return pl.BlockSpec((None,) + width_dims, lambda b, c, pt: (pt[b * n_pages + c * pages + p], 0, 0))

    per_b = lambda shape: pl.BlockSpec((None,) + shape, lambda b, c, pt: (b,) + (0,) * len(shape))
    in_specs = [pl.BlockSpec((MLA_HEADS, None, t_new, MLA_KV_RANK), lambda b, c, pt: (0, b, 0, 0)),
                pl.BlockSpec((MLA_HEADS, None, t_new, MLA_ROPE), lambda b, c, pt: (0, b, 0, 0)),
                per_b((t_new, MLA_KV_RANK)), per_b((t_new, MLA_ROPE)), per_b((MLA_HEADS, 128)),
                pl.BlockSpec(wuv_h.shape, lambda b, c, pt: (0, 0, 0))]
    in_specs += [pool_spec((page, MLA_KV_RANK), p) for p in range(pages)]
    in_specs += [pool_spec((page, MLA_ROPE), p) for p in range(pages)]
    in_specs += [pool_spec((MLA_HEADS, page), p) for p in range(pages)]
    rows = MLA_HEADS * t_new
    grid_spec = pltpu.PrefetchScalarGridSpec(
        num_scalar_prefetch=1,
        grid=(bs, n_pages // pages),
        in_specs=in_specs,
        out_specs=pl.BlockSpec((None, t_new, MLA_HEADS * MLA_V), lambda b, c, pt: (b, 0, 0)),
        scratch_shapes=[pltpu.VMEM((rows, 1), F32), pltpu.VMEM((rows, 1), F32), pltpu.VMEM((rows, MLA_KV_RANK), F32)],
    )
    return pl.pallas_call(
        functools.partial(_decode_body, pages=pages, t_new=t_new),
        grid_spec=grid_spec,
        out_shape=jax.ShapeDtypeStruct((bs, t_new, MLA_HEADS * MLA_V), F32),
        compiler_params=_cparams("parallel", "arbitrary"),
        name="mla_sample_attn",
    )(page_table.reshape(-1), ql, qr, c_new, kr_new, ir_new_t, wuv_h,
      *([lat_pool] * pages), *([kr_pool] * pages), *([irt_pool] * pages))


def _gla_body(gq_ref, gk_ref, gv_ref, r_ref, alr_ref, wga_ref, bga_ref, gn_ref, *rest, chunk, sub, nb, carry):
    if carry:
        y_ref, s_out_ref = rest
        s_in_ref = s_out_ref

        @pl.when(pl.program_id(0) == 0)
        def _():
            s_out_ref[...] = jnp.zeros(s_out_ref.shape, F32)
    else:
        s_in_ref, y_ref, s_out_ref = rest
    row = lax.broadcasted_iota(jnp.int32, (chunk, chunk), 0)
    col = lax.broadcasted_iota(jnp.int32, (chunk, chunk), 1)
    tri = (col <= row).astype(F32)
    ones_cv = jnp.ones((chunk, GLA_DV), F32)
    brow = lax.broadcasted_iota(jnp.int32, (sub, chunk), 0)
    bcol = lax.broadcasted_iota(jnp.int32, (sub, chunk), 1)
    hi = lax.Precision.HIGHEST

    def one_batch(bi, carry_):
        alr = alr_ref[bi].astype(BF16)
        for h in range(GLA_HEADS):
            ks = slice(h * GLA_DK, (h + 1) * GLA_DK)
            vs = slice(h * GLA_DV, (h + 1) * GLA_DV)
            q = gq_ref[bi, :, ks] * GLA_DK ** -0.5
            k = gk_ref[bi, :, ks]
            v = gv_ref[bi, :, vs]
            vb = v.astype(BF16)
            a = _dot(alr, wga_ref[:, ks]) + bga_ref[:, ks]
            g = (jnp.minimum(a, 0.0) - jnp.log1p(jnp.exp(-jnp.abs(a)))) / GLA_TAU
            bc = _dot(tri, g, precision=hi)
            b_last = bc[chunk - 1:chunk]
            s0 = s_in_ref[bi, h]
            o = _dot((q * jnp.exp(bc)).astype(BF16), s0.astype(BF16))
            gn = gn_ref[...]
            for blk in range(chunk // sub):
                r0 = blk * sub
                qb = q[r0:r0 + sub]
                bb = bc[r0:r0 + sub]
                a_blk = jnp.zeros((sub, chunk), F32)
                for s in range(sub):
                    e = jnp.exp(jnp.minimum(bb - bb[s:s + 1], 0.0))
                    cs = jnp.sum(qb * e * k[r0 + s:r0 + s + 1], -1, keepdims=True)
                    a_blk = jnp.where(bcol == r0 + s, jnp.where(brow >= s, cs, 0.0), a_blk)
                if blk > 0:
                    b0 = bc[r0 - 1:r0]
                    qs = (qb * jnp.exp(bb - b0)).astype(BF16)
                    ksc = (k * jnp.exp(jnp.minimum(b0 - bc, 0.0))).astype(BF16)
                    a_blk = jnp.where(bcol < r0, _dot_nt(qs, ksc), a_blk)
                ob = o[r0:r0 + sub] + _dot(a_blk.astype(BF16), vb)
                rr = r_ref[bi, r0:r0 + sub, vs]
                y_ref[bi, r0:r0 + sub, vs] = _rms(ob, gn) * (rr * jax.nn.sigmoid(rr))
            dec = jnp.exp(_dot_tn(g, ones_cv, precision=hi))
            kd = (k * jnp.exp(b_last - bc)).astype(BF16)
            s_out_ref[bi, h] = dec * s0 + _dot_tn(kd, vb)
        return carry_

    lax.fori_loop(0, nb, one_batch, 0)


def _gla(z3, state, wga, bga, gn, chunk, sub, nb):
    b, s, _ = z3.shape
    carry = state is None
    if carry:
        grid = (s // chunk,)
        zmap = lambda blk: (lambda c: (0, c, blk))
        smap = lambda c: (0, 0, 0, 0)
        nb = b
    else:
        grid = (b // nb,)
        zmap = lambda blk: (lambda i: (i, 0, blk))
        smap = lambda i: (i, 0, 0, 0)
    zspec = lambda wb: pl.BlockSpec((nb, chunk, wb[0]), zmap(wb[1]))
    const = lambda shape: pl.BlockSpec(shape, lambda i: (0,) * len(shape))
    sspec = pl.BlockSpec((nb, GLA_HEADS, GLA_DK, GLA_DV), smap)
    in_specs = [zspec(ZB_GQ), zspec(ZB_GK), zspec(ZB_GV), zspec(ZB_R), zspec(ZB_ALR),
                const(wga.shape), const(bga.shape), const(gn.shape)]
    args = [z3, z3, z3, z3, z3, wga, bga, gn]
    if not carry:
        in_specs.append(sspec)
        args.append(state)
    return pl.pallas_call(
        functools.partial(_gla_body, chunk=chunk, sub=sub, nb=nb, carry=carry),
        grid=grid,
        in_specs=in_specs,
        out_specs=[pl.BlockSpec((nb, chunk, GLA_HEADS * GLA_DV), zmap(0)), sspec],
        out_shape=[jax.ShapeDtypeStruct((b, s, GLA_HEADS * GLA_DV), F32),
                   jax.ShapeDtypeStruct((b, GLA_HEADS, GLA_DK, GLA_DV), F32)],
        compiler_params=_cparams("arbitrary"),
        name="gla_prompt" if carry else "gla_sample",
    )(*args)


def _out_proj_body(mla_ref, gla_ref, x_ref, nm_ref, w_ref, o_ref):
    half = mla_ref.shape[1]
    mn = _rms(mla_ref[...], nm_ref[...]).astype(BF16)
    o_ref[...] = x_ref[...] + _dot(mn, w_ref[:half]) + _dot(gla_ref[...].astype(BF16), w_ref[half:])


def _out_proj(mla_o, gla_y, x, nm, w, tm):
    n, d = x.shape
    half = mla_o.shape[1]
    return pl.pallas_call(
        _out_proj_body,
        grid=(n // tm,),
        in_specs=[pl.BlockSpec((tm, half), lambda i: (i, 0)), pl.BlockSpec((tm, half), lambda i: (i, 0)),
                  pl.BlockSpec((tm, d), lambda i: (i, 0)), pl.BlockSpec((1, half), lambda i: (0, 0)),
                  pl.BlockSpec(w.shape, lambda i: (0, 0))],
        out_specs=pl.BlockSpec((tm, d), lambda i: (i, 0)),
        out_shape=jax.ShapeDtypeStruct((n, d), F32),
        compiler_params=_cparams("parallel"),
        name="out_proj",
    )(mla_o, gla_y, x, nm.reshape(1, half), w)


def _top_desc(s, n):
    vals = []
    w = s
    for _ in range(n):
        m = jnp.max(w, axis=0, keepdims=True)
        vals.append(m)
        w = jnp.where(w == m, NEG_INF, w)
    return vals


def _peer_route_body(h_ref, g_ref, wqt_ref, k1_ref, k2_ref, xnt_ref, s2_ref, e2_ref, thr_ref, e1z_ref):
    xnt = _rms(h_ref[...], g_ref[...]).T.astype(BF16)
    xnt_ref[...] = xnt
    qt = _dot(wqt_ref[...], xnt).astype(BF16)
    half = PEER_N_KEYS
    for h in range(PEER_HEADS):
        s1 = _dot(k1_ref[...], qt[(2 * h) * half:(2 * h + 1) * half])
        s2 = _dot(k2_ref[...], qt[(2 * h + 1) * half:(2 * h + 2) * half])
        v1 = _top_desc(s1, PEER_TOPK)
        v2 = _top_desc(s2, PEER_TOPK)
        pairs = [(a, b) for a in range(PEER_TOPK) for b in range(PEER_TOPK // (a + 1))]
        sums = {ab: v1[ab[0]] + v2[ab[1]] for ab in pairs}
        npad = -len(pairs) % 8
        stack = jnp.concatenate([sums[ab] for ab in pairs] + [jnp.full_like(v1[0], NEG_INF)] * npad, 0)
        tau = _top_desc(stack, PEER_TOPK)[-1]
        top = v1[0] + v2[0]
        zsum = jnp.sum(jnp.where(stack >= tau, jnp.exp(stack - top), 0.0), axis=0, keepdims=True)
        thr = jnp.full(s1.shape, POS_INF, F32)
        for a in range(PEER_TOPK):
            ta = jnp.full_like(tau, POS_INF)
            for b in range(PEER_TOPK // (a + 1)):
                ta = jnp.minimum(ta, jnp.where(sums[(a, b)] >= tau, v2[b], POS_INF))
            thr = jnp.where(s1 == v1[a], ta, thr)
        s2_ref[h] = s2
        e2_ref[h] = jnp.exp(s2 - v2[0])
        thr_ref[h] = thr
        e1z_ref[h] = jnp.exp(s1 - v1[0]) / zsum


def _peer_route(h, g, wqt, k1, k2, tt):
    n, d = h.shape
    per_key = jax.ShapeDtypeStruct((PEER_HEADS, PEER_N_KEYS, n), F32)
    kspec = pl.BlockSpec((PEER_HEADS, PEER_N_KEYS, tt), lambda i: (0, 0, i))
    return pl.pallas_call(
        _peer_route_body,
        grid=(n // tt,),
        in_specs=[pl.BlockSpec((tt, d), lambda i: (i, 0)), pl.BlockSpec((1, d), lambda i: (0, 0)),
                  pl.BlockSpec(wqt.shape, lambda i: (0, 0)), pl.BlockSpec(k1.shape, lambda i: (0, 0)),
                  pl.BlockSpec(k2.shape, lambda i: (0, 0))],
        out_specs=[pl.BlockSpec((d, tt), lambda i: (0, i)), kspec, kspec, kspec, kspec],
        out_shape=[jax.ShapeDtypeStruct((d, n), BF16), per_key, per_key, per_key, per_key],
        compiler_params=_cparams("parallel"),
        name="peer_route",
    )(h, g.reshape(1, d), wqt, k1, k2)


def _peer_dense_body(xnt_ref, u_ref, vt_ref, s2_ref, e2_ref, thr_ref, e1z_ref, yt_ref, ca_sc):
    @pl.when(pl.program_id(1) == 0)
    def _():
        yt_ref[...] = jnp.zeros(yt_ref.shape, F32)

    act = _dot(u_ref[...], xnt_ref[...])
    act = 0.5 * act * (1.0 + lax.erf(act * (2.0 ** -0.5)))
    for gi in range(u_ref.shape[0] // PEER_N_KEYS):
        coef = jnp.zeros((PEER_N_KEYS, act.shape[1]), F32)
        for h in range(PEER_HEADS):
            sel = jnp.where(s2_ref[h] >= thr_ref[h, gi:gi + 1, :], e2_ref[h], 0.0)
            coef = coef + sel * e1z_ref[h, gi:gi + 1, :]
        rows = slice(gi * PEER_N_KEYS, (gi + 1) * PEER_N_KEYS)
        ca_sc[rows] = (coef * act[rows]).astype(BF16)
    yt_ref[...] += _dot(vt_ref[...], ca_sc[...])


def _peer_dense(xnt, u, vt, s2, e2, thr, e1z, tt, te):
    d, n = xnt.shape
    ne = u.shape[0]
    gpt = te // PEER_N_KEYS
    full = pl.BlockSpec((PEER_HEADS, PEER_N_KEYS, tt), lambda i, j: (0, 0, i))
    part = pl.BlockSpec((PEER_HEADS, gpt, tt), lambda i, j: (0, j, i))
    return pl.pallas_call(
        _peer_dense_body,
        grid=(n // tt, ne // te),
        in_specs=[pl.BlockSpec((d, tt), lambda i, j: (0, i)), pl.BlockSpec((te, d), lambda i, j: (j, 0)),
                  pl.BlockSpec((d, te), lambda i, j: (0, j)), full, full, part, part],
        out_specs=pl.BlockSpec((d, tt), lambda i, j: (0, i)),
        out_shape=jax.ShapeDtypeStruct((d, n), F32),
        scratch_shapes=[pltpu.VMEM((te, tt), BF16)],
        compiler_params=_cparams("parallel", "arbitrary"),
        name="peer_dense",
    )(xnt, u, vt, s2, e2, thr, e1z)


def _ple_body(h_ref, yt_ref, p_ref, g_ref, wg_ref, wp_ref, o_ref):
    h2 = h_ref[...] + yt_ref[...].T
    gate = jax.nn.sigmoid(_dot(_rms(h2, g_ref[...]).astype(BF16), wg_ref[...]))
    o_ref[...] = h2 + gate * _dot(p_ref[...].astype(BF16), wp_ref[...])


def _ple(h, yt, p, g, wg, wp, tm):
    n, d = h.shape
    return pl.pallas_call(
        _ple_body,
        grid=(n // tm,),
        in_specs=[pl.BlockSpec((tm, d), lambda i: (i, 0)), pl.BlockSpec((d, tm), lambda i: (0, i)),
                  pl.BlockSpec((tm, p.shape[1]), lambda i: (i, 0)), pl.BlockSpec((1, d), lambda i: (0, 0)),
                  pl.BlockSpec(wg.shape, lambda i: (0, 0)), pl.BlockSpec(wp.shape, lambda i: (0, 0))],
        out_specs=pl.BlockSpec((tm, d), lambda i: (i, 0)),
        out_shape=jax.ShapeDtypeStruct((n, d), F32),
        compiler_params=_cparams("parallel"),
        name="ple",
    )(h, yt, p, g.reshape(1, d), wg, wp)


def _rope_table(pos):
    half = MLA_ROPE // 2
    inv_freq = ROPE_THETA ** (-jnp.arange(half, dtype=F32) / half)
    ang = pos.astype(F32)[:, None] * inv_freq
    cos, sin = jnp.cos(ang), jnp.sin(ang)
    return jnp.concatenate([cos, cos, sin, sin], -1)


def _tile(n, t):
    return t if n % t == 0 else n


def _finish(x2, mla_o, gla_y, p2, w):
    n = x2.shape[0]
    h = _out_proj(mla_o, gla_y, x2, w["norm_mla_out"], w["w_out"], _tile(n, TM_OUT))
    xnt, s2, e2, thr, e1z = _peer_route(h, w["norm_ffn"], w["wqt"], w["k1"], w["k2"], _tile(n, TT_ROUTE))
    yt = _peer_dense(xnt, w["u"], w["vt"], s2, e2, thr, e1z, _tile(n, TT_DENSE), TE_DENSE)
    return _ple(h, yt, p2, w["norm_ple"], w["w_ple_gate"], w["w_ple_proj"], _tile(n, TM_PLE))


def kernel(x_prompt, x_sample, cache_kv_latent, cache_k_rope, cache_k_inv_rms, state_gla, page_table, p_prompt,
           p_sample, norm_mix, w_in, norm_q_lat, w_uq, norm_kv_lat, w_uk, w_uv, qk_gain_q, qk_gain_k, norm_mla_out,
           w_gla_a, b_gla_a, norm_gla_out, w_out, norm_ffn, peer_w_q, peer_keys1, peer_keys2, peer_u, peer_v,
           norm_ple, w_ple_gate, w_ple_proj):
    depth = w_in.shape[0]
    b, s, d = x_prompt.shape
    bs, t_new, _ = x_sample.shape
    n_pages = page_table.shape[1]
    page = cache_kv_latent.shape[2]
    past_len = n_pages * page
    tab_p = _rope_table(jnp.arange(s))
    tab_s = _rope_table(past_len + jnp.arange(t_new))
    hp, hs = x_prompt.reshape(b * s, d), x_sample.reshape(bs * t_new, d)
    outs = [[] for _ in range(8)]
    for i in range(depth):
        wi = w_in[i]
        cuts = [0, 512, 768, 832, 1344, 1856, 2880, 3904, 3920]
        cq_w, ckv_w, kr_w, gq_w, gk_w, gv_w, r_w, alr_w = [wi[:, cuts[j]:cuts[j + 1]] for j in range(8)]
        w_in_ext = jnp.concatenate(
            [cq_w, gq_w, gk_w, ckv_w, kr_w, _rot_cols(kr_w), alr_w,
             jnp.zeros((d, 128 - GLA_GATE_RANK), F32), gv_w, r_w], 1).astype(BF16)
        wuq3 = w_uq[i].reshape(MLA_Q_RANK, MLA_HEADS, MLA_QK)
        wuq_ext = jnp.concatenate([wuq3, _rot_cols(wuq3[..., MLA_NOPE:])], -1).reshape(MLA_Q_RANK, -1).astype(BF16)
        gq, gk = qk_gain_q[i], qk_gain_k[i]
        gq_ext = jnp.concatenate([gq, _rot_gain(gq[MLA_NOPE:])]).reshape(1, -1)
        gk_ext = jnp.concatenate([gk, _rot_gain(gk[MLA_NOPE:])]).reshape(1, -1)
        wuk = w_uk[i].reshape(MLA_KV_RANK, -1).astype(BF16)
        wuv = w_uv[i].reshape(MLA_KV_RANK, -1).astype(BF16)
        wuk_t = jnp.transpose(w_uk[i], (1, 2, 0)).astype(BF16)
        wuv_h = jnp.transpose(w_uv[i], (1, 0, 2)).astype(BF16)
        wga = jnp.concatenate([w_gla_a[i], jnp.zeros((128 - GLA_GATE_RANK, w_gla_a.shape[2]), F32)], 0).astype(BF16)
        bga = b_gla_a[i].reshape(1, -1)
        gn = norm_gla_out[i].reshape(1, -1)
        w = {"norm_mla_out": norm_mla_out[i], "w_out": w_out[i].astype(BF16), "norm_ffn": norm_ffn[i],
             "wqt": peer_w_q[i].T.astype(BF16), "k1": peer_keys1[i].astype(BF16), "k2": peer_keys2[i].astype(BF16),
             "u": peer_u[i].astype(BF16), "vt": peer_v[i].T.astype(BF16), "norm_ple": norm_ple[i],
             "w_ple_gate": w_ple_gate[i].astype(BF16), "w_ple_proj": w_ple_proj[i].astype(BF16)}
        prep = (norm_q_lat[i], norm_kv_lat[i], gq_ext, gk_ext, wuq_ext, wuk)

        n = b * s
        z = _norm_matmul(hp, norm_mix[i], w_in_ext, _tile(n, TM_IN), TN_IN)
        tm = _tile(s, TM_PREP)
        qt, kt, v, lat, kr, ir = _mla_prep(z, tab_p, s // tm, *prep, wuv, tm, sample=False)
        tq = _tile(s, FLASH_TILE)
        mla_o = _flash(qt.reshape(b, s, -1), kt.reshape(b, s, -1), v.reshape(b, s, -1), tq, tq)
        gla_y, s_fin = _gla(z.reshape(b, s, Z_COLS), None, wga, bga, gn, GLA_CHUNK, GLA_SUB, b)
        hp_next = _finish(hp, mla_o.reshape(n, -1), gla_y.reshape(n, -1), p_prompt[i].reshape(n, -1), w)
        outs[0].append(lat.reshape(b, s, -1))
        outs[1].append(kr.reshape(b, s, -1))
        outs[2].append(ir[:, :MLA_HEADS].reshape(b, s, -1))
        outs[3].append(s_fin)

        n = bs * t_new
        z = _norm_matmul(hs, norm_mix[i], w_in_ext, _tile(n, TM_IN), TN_IN)
        tm = _tile(n, TM_PREP)
        tab = jnp.tile(tab_s, (tm // t_new, 1))
        ql, qr, lat, kr, ir = _mla_prep(z, tab, 1, *prep, wuk_t, tm, sample=True)
        ir_new_t = jnp.transpose(ir.reshape(bs, t_new, 128)[:, :, :MLA_HEADS], (0, 2, 1))
        ir_new_t = jnp.pad(ir_new_t, ((0, 0), (0, 0), (0, 128 - t_new)))
        irt_pool = jnp.transpose(cache_k_inv_rms[i], (0, 2, 1))
        mla_o = _decode(page_table, ql.reshape(MLA_HEADS, bs, t_new, -1), qr.reshape(MLA_HEADS, bs, t_new, -1),
                        lat.reshape(bs, t_new, -1), kr.reshape(bs, t_new, -1), ir_new_t, wuv_h,
                        cache_kv_latent[i], cache_k_rope[i], irt_pool, _tile(n_pages, DECODE_PAGES))
        gla_y, s_new = _gla(z.reshape(bs, t_new, Z_COLS), state_gla[i], wga, bga, gn, t_new, t_new,
                            _tile(bs, GLA_SAMPLE_ROWS))
        hs_next = _finish(hs, mla_o.reshape(n, -1), gla_y.reshape(n, -1), p_sample[i].reshape(n, -1), w)
        outs[4].append(lat.reshape(bs, t_new, -1))
        outs[5].append(kr.reshape(bs, t_new, -1))
        outs[6].append(ir[:, :MLA_HEADS].reshape(bs, t_new, -1))
        outs[7].append(s_new)
        hp, hs = hp_next, hs_next
    return (hp.reshape(b, s, d), hs.reshape(bs, t_new, d)) + tuple(jnp.stack(o, 0) for o in outs)
```

```python
import functools

import jax
import jax.numpy as jnp
from jax import lax
from jax.experimental import pallas as pl
from jax.experimental.pallas import tpu as pltpu

F32 = jnp.float32
BF16 = jnp.bfloat16
EPS = 1e-6
NEG_INF = float("-inf")
POS_INF = float("inf")

MLA_HEADS = 8
MLA_NOPE = 128
MLA_ROPE = 64
MLA_QK = MLA_NOPE + MLA_ROPE
MLA_V = 128
MLA_Q_RANK = 512
MLA_KV_RANK = 256
MLA_SCALE = MLA_QK ** -0.5
LOG2E = 1.4426950408889634
MLA_HEAD_PAD = 256
ROPE_THETA = 10000.0
GLA_HEADS = 4
GLA_DK = 128
GLA_DV = 256
GLA_GATE_RANK = 16
GLA_TAU = 16.0
GLA_CHUNK = 64
GLA_SUB = 16
PEER_HEADS = 8
PEER_N_KEYS = 128
PEER_TOPK = 16
PEER_SUB = 32
PLE_DIM = 256

Z_COLS = 4096
ZB_CQ = (512, 0)
ZB_GQ = (512, 1)
ZB_GK = (512, 2)
ZB_CKV = (256, 6)
ZB_KR = (128, 14)
ZB_ALR = (128, 15)
ZB_GV = (1024, 2)
ZB_R = (1024, 3)

V7X_VMEM_LIMIT = 56 * 1024 * 1024

TM_IN, TN_IN = 512, 1024
TM_PREP = 256
FLASH_TILE = 512
FLASH_HEADS = 8
FLASH_SUB = 256
DECODE_PAGES = 32
GLA_SAMPLE_ROWS = 8
TM_OUT = 256
TT_ROUTE = 256
TT_DENSE, TE_DENSE = 512, 1024
TM_PLE = 256


def _cparams(*sem):
    return pltpu.CompilerParams(dimension_semantics=sem, vmem_limit_bytes=V7X_VMEM_LIMIT)


def _dot(a, b, **kw):
    return jnp.dot(a, b, preferred_element_type=F32, **kw)


def _dot_nt(a, b):
    return lax.dot_general(a, b, (((1,), (1,)), ((), ())), preferred_element_type=F32)


def _dot_tn(a, b, **kw):
    return lax.dot_general(a, b, (((0,), (0,)), ((), ())), preferred_element_type=F32, **kw)


def _rms(x, g):
    return x * lax.rsqrt(jnp.mean(x * x, -1, keepdims=True) + EPS) * g


def _rot_cols(w):
    h = w.shape[-1] // 2
    return jnp.concatenate([-w[..., h:], w[..., :h]], -1)


def _rot_gain(g):
    h = g.shape[-1] // 2
    return jnp.concatenate([g[..., h:], g[..., :h]], -1)


def _norm_matmul_body(x_ref, g_ref, w_ref, o_ref, xn_sc):
    @pl.when(pl.program_id(1) == 0)
    def _():
        xn_sc[...] = _rms(x_ref[...], g_ref[...]).astype(BF16)

    o_ref[...] = _dot(xn_sc[...], w_ref[...])


def _norm_matmul(x, g, w, tm, tn):
    n, d = x.shape
    nc = w.shape[1]
    return pl.pallas_call(
        _norm_matmul_body,
        grid=(n // tm, nc // tn),
        in_specs=[pl.BlockSpec((tm, d), lambda i, j: (i, 0)),
                  pl.BlockSpec((1, d), lambda i, j: (0, 0)),
                  pl.BlockSpec((d, tn), lambda i, j: (0, j))],
        out_specs=pl.BlockSpec((tm, tn), lambda i, j: (i, j)),
        out_shape=jax.ShapeDtypeStruct((n, nc), F32),
        scratch_shapes=[pltpu.VMEM((tm, d), BF16)],
        compiler_params=_cparams("parallel", "arbitrary"),
        name="in_proj",
    )(x, g.reshape(1, d), w)


def _mla_prep_body(cq_ref, ckv_ref, krb_ref, tab_ref, nq_ref, nkv_ref, gq_ref, gk_ref, wuq_ref, wuk_ref, wx_ref,
                   *outs, sample):
    if sample:
        ql_ref, qr_ref, lat_ref, kr_ref, ir_ref = outs
    else:
        qt_ref, kt_ref, vt_ref, lat_ref, kr_ref, ir_ref = outs
    tab = tab_ref[...]
    gq = gq_ref[...]
    gk = gk_ref[...]
    cqn = _rms(cq_ref[...], nq_ref[...]).astype(BF16)
    qraw = _dot(cqn, wuq_ref[...])
    c = _rms(ckv_ref[...], nkv_ref[...])
    lat_ref[...] = c
    cb = c.astype(BF16)
    kn = _dot(cb, wuk_ref[...])
    if not sample:
        vt_ref[...] = _dot(cb, wx_ref[...]).T.astype(BF16)
    krb = krb_ref[...]
    lane = lax.broadcasted_iota(jnp.int32, krb.shape, 1)
    ss_kr = jnp.sum(jnp.where(lane < MLA_ROPE, krb * krb, 0.0), -1, keepdims=True)
    t = krb * gk[:, MLA_NOPE:] * tab
    kr2 = t + pltpu.roll(t, MLA_ROPE, 1)
    kr_ref[...] = kr2[:, :MLA_ROPE]
    krz = jnp.where(lane < MLA_ROPE, kr2, 0.0)
    lane_q = lax.broadcasted_iota(jnp.int32, (krb.shape[0], MLA_HEAD_PAD), 1)
    ir_acc = jnp.zeros(krb.shape, F32)
    for h in range(MLA_HEADS):
        knh = kn[:, h * MLA_NOPE:(h + 1) * MLA_NOPE]
        ss = jnp.sum(knh * knh, -1, keepdims=True) + ss_kr
        inv = lax.rsqrt(ss / MLA_QK + EPS)
        ir_acc = jnp.where(lane == h, inv, ir_acc)
        qh = qraw[:, h * MLA_HEAD_PAD:(h + 1) * MLA_HEAD_PAD]
        ssq = jnp.sum(jnp.where(lane_q < MLA_QK, qh * qh, 0.0), -1, keepdims=True)
        qinv = lax.rsqrt(ssq / MLA_QK + EPS)
        qn = qh[:, :MLA_NOPE] * qinv * gq[:, :MLA_NOPE]
        tq = qh[:, MLA_NOPE:] * qinv * gq[:, MLA_NOPE:] * tab
        qr2 = tq + pltpu.roll(tq, MLA_ROPE, 1)
        if sample:
            qg = (qn * gk[:, :MLA_NOPE]).astype(BF16)
            ql_ref[h] = _dot(qg, wx_ref[h]) * MLA_SCALE
            qr_ref[h] = qr2[:, :MLA_ROPE] * MLA_SCALE
        else:
            lo = h * MLA_HEAD_PAD
            qt_ref[:, lo:lo + MLA_NOPE] = (qn * (MLA_SCALE * LOG2E)).astype(BF16)
            qt_ref[:, lo + MLA_NOPE:lo + MLA_HEAD_PAD] = (qr2 * (MLA_SCALE * LOG2E)).astype(BF16)
            kt_ref[:, lo:lo + MLA_NOPE] = (knh * gk[:, :MLA_NOPE] * inv).astype(BF16)
            kt_ref[:, lo + MLA_NOPE:lo + MLA_HEAD_PAD] = (krz * inv).astype(BF16)
    ir_ref[...] = ir_acc


def _mla_prep(z, tab, tab_blocks, nq, nkv, gq_ext, gk_ext, wuq_ext, wuk, wx, tm, sample):
    n = z.shape[0]
    hp = MLA_HEADS * MLA_HEAD_PAD

    def const(shape):
        return pl.BlockSpec(shape, lambda i: (0,) * len(shape))

    in_specs = [pl.BlockSpec((tm, ZB_CQ[0]), lambda i: (i, ZB_CQ[1])),
                pl.BlockSpec((tm, ZB_CKV[0]), lambda i: (i, ZB_CKV[1])),
                pl.BlockSpec((tm, ZB_KR[0]), lambda i: (i, ZB_KR[1])),
                pl.BlockSpec((tm, 128), lambda i: (i % tab_blocks, 0)),
                const((1, MLA_Q_RANK)), const((1, MLA_KV_RANK)), const((1, MLA_HEAD_PAD)), const((1, MLA_HEAD_PAD)),
                const(wuq_ext.shape), const(wuk.shape), const(wx.shape)]
    row = lambda w: pl.BlockSpec((tm, w), lambda i: (i, 0))
    tail_specs = [row(MLA_KV_RANK), row(MLA_ROPE), row(128)]
    tail_shapes = [jax.ShapeDtypeStruct((n, MLA_KV_RANK), F32), jax.ShapeDtypeStruct((n, MLA_ROPE), F32),
                   jax.ShapeDtypeStruct((n, 128), F32)]
    if sample:
        head = lambda w: pl.BlockSpec((MLA_HEADS, tm, w), lambda i: (0, i, 0))
        out_specs = [head(MLA_KV_RANK), head(MLA_ROPE)] + tail_specs
        out_shape = [jax.ShapeDtypeStruct((MLA_HEADS, n, MLA_KV_RANK), F32),
                     jax.ShapeDtypeStruct((MLA_HEADS, n, MLA_ROPE), F32)] + tail_shapes
    else:
        out_specs = [row(hp), row(hp), pl.BlockSpec((MLA_HEADS * MLA_V, tm), lambda i: (0, i))] + tail_specs
        out_shape = [jax.ShapeDtypeStruct((n, hp), BF16), jax.ShapeDtypeStruct((n, hp), BF16),
                     jax.ShapeDtypeStruct((MLA_HEADS * MLA_V, n), BF16)] + tail_shapes
    return pl.pallas_call(
        functools.partial(_mla_prep_body, sample=sample),
        grid=(n // tm,),
        in_specs=in_specs, out_specs=out_specs, out_shape=out_shape,
        compiler_params=_cparams("parallel"),
        name="mla_prep_sample" if sample else "mla_prep_prompt",
    )(z, z, z, tab, nq.reshape(1, -1), nkv.reshape(1, -1), gq_ext, gk_ext, wuq_ext, wuk, wx)


def _flash_body(q_ref, k_ref, vt_ref, o_ref, m_sc, l_sc, acc_sc, *, heads, sub):
    i = pl.program_id(2)
    j = pl.program_id(3)

    @pl.when(j == 0)
    def _():
        m_sc[...] = jnp.full(m_sc.shape, NEG_INF, F32)
        l_sc[...] = jnp.zeros(l_sc.shape, F32)
        acc_sc[...] = jnp.zeros(acc_sc.shape, F32)

    tile = q_ref.shape[0]

    def step(diagonal):
        def q_cols(c, carry):
            r0 = pl.multiple_of(c * sub, sub)
            cols = pl.ds(r0, sub)
            vrows = [slice(h * MLA_V, (h + 1) * MLA_V) for h in range(heads)]
            prev = [(m_sc[h, :, cols], l_sc[h, :, cols], acc_sc[vrows[h], cols]) for h in range(heads)]
            new = []

            def scores(h):
                qk = slice(h * MLA_HEAD_PAD, (h + 1) * MLA_HEAD_PAD)
                return _dot_nt(k_ref[:, qk], q_ref[cols, qk])

            ahead = 2
            pending = [scores(h) for h in range(min(ahead, heads))]
            for h in range(heads):
                s = pending.pop(0)
                if diagonal:
                    keep = (lax.broadcasted_iota(jnp.int32, s.shape, 0)
                            <= r0 + lax.broadcasted_iota(jnp.int32, s.shape, 1))
                    s = jnp.where(keep, s, NEG_INF)
                m_prev, l_prev, acc_prev = prev[h]
                m_new = jnp.maximum(m_prev, jnp.max(s, 0, keepdims=True))
                alpha = jnp.exp2(m_prev - m_new)
                p = jnp.exp2(s - m_new)
                l_new = alpha * l_prev + jnp.sum(p, 0, keepdims=True)
                acc = alpha * acc_prev + _dot(vt_ref[vrows[h], :], p.astype(BF16))
                new.append((m_new, l_new, acc))
                if h + ahead < heads:
                    pending.append(scores(h + ahead))
            for h, (m_new, l_new, acc) in enumerate(new):
                if diagonal:
                    o_ref[vrows[h], cols] = acc / l_new
                else:
                    m_sc[h, :, cols] = m_new
                    l_sc[h, :, cols] = l_new
                    acc_sc[vrows[h], cols] = acc
            return carry

        lax.fori_loop(0, tile // sub, q_cols, 0)

    pl.when(j < i)(lambda: step(False))
    pl.when(j == i)(lambda: step(True))


def _flash(qt, kt, vt, tile, heads):
    b, s, _ = qt.shape
    nt = s // tile
    return pl.pallas_call(
        functools.partial(_flash_body, heads=heads, sub=min(FLASH_SUB, tile)),
        grid=(b, MLA_HEADS // heads, nt, nt),
        in_specs=[pl.BlockSpec((None, tile, heads * MLA_HEAD_PAD), lambda bb, h, i, j: (bb, i, h)),
                  pl.BlockSpec((None, tile, heads * MLA_HEAD_PAD), lambda bb, h, i, j: (bb, jnp.minimum(j, i), h)),
                  pl.BlockSpec((heads * MLA_V, tile), lambda bb, h, i, j: (h, bb * nt + jnp.minimum(j, i)))],
        out_specs=pl.BlockSpec((None, heads * MLA_V, tile), lambda bb, h, i, j: (bb, h, i)),
        out_shape=jax.ShapeDtypeStruct((b, MLA_HEADS * MLA_V, s), F32),
        scratch_shapes=[pltpu.VMEM((heads, 1, tile), F32), pltpu.VMEM((heads, 1, tile), F32),
                        pltpu.VMEM((heads * MLA_V, tile), F32)],
        compiler_params=_cparams("parallel", "parallel", "parallel", "arbitrary"),
        name="mla_prompt_attn",
    )(qt, kt, vt)


def _decode_body(pt_ref, ql_ref, qr_ref, cn_ref, krn_ref, irn_ref, wuv_ref, *rest, pages, t_new):
    lat_refs = rest[:pages]
    kr_refs = rest[pages:2 * pages]
    ir_refs = rest[2 * pages:3 * pages]
    o_ref, m_sc, l_sc, acc_sc = rest[3 * pages:]
    rows = MLA_HEADS * t_new
    c = pl.program_id(1)

    @pl.when(c == 0)
    def _():
        m_sc[...] = jnp.full(m_sc.shape, NEG_INF, F32)
        l_sc[...] = jnp.zeros(l_sc.shape, F32)
        acc_sc[...] = jnp.zeros(acc_sc.shape, F32)

    ql = ql_ref[...].reshape(rows, MLA_KV_RANK).astype(BF16)
    qr = qr_ref[...].reshape(rows, MLA_ROPE).astype(BF16)

    def attend(lat_b, kr_b, ir_b, mask=None):
        s = _dot_nt(ql, lat_b) + _dot_nt(qr, kr_b)
        ir_t = ir_b.T
        s = (s.reshape(MLA_HEADS, t_new, -1) * ir_t[:, None, :]).reshape(rows, -1)
        if mask is not None:
            s = jnp.where(mask, s, NEG_INF)
        m_prev = m_sc[...]
        m_new = jnp.maximum(m_prev, jnp.max(s, -1, keepdims=True))
        alpha = jnp.exp(m_prev - m_new)
        p = jnp.exp(s - m_new)
        l_sc[...] = alpha * l_sc[...] + jnp.sum(p, -1, keepdims=True)
        acc_sc[...] = alpha * acc_sc[...] + _dot(p.astype(BF16), lat_b)
        m_sc[...] = m_new

    attend(jnp.concatenate([r[...].astype(BF16) for r in lat_refs], 0),
           jnp.concatenate([r[...].astype(BF16) for r in kr_refs], 0),
           jnp.concatenate([r[...] for r in ir_refs], 0))

    @pl.when(c == pl.num_programs(1) - 1)
    def _():
        pad = 128 - t_new
        lb = jnp.concatenate([cn_ref[...], jnp.zeros((pad, MLA_KV_RANK), F32)], 0).astype(BF16)
        kb = jnp.concatenate([krn_ref[...], jnp.zeros((pad, MLA_ROPE), F32)], 0).astype(BF16)
        ib = jnp.concatenate([irn_ref[:, :MLA_HEADS], jnp.zeros((pad, MLA_HEADS), F32)], 0)
        tq = lax.rem(lax.broadcasted_iota(jnp.int32, (rows, 128), 0), t_new)
        key = lax.broadcasted_iota(jnp.int32, (rows, 128), 1)
        attend(lb, kb, ib, mask=key <= tq)
        o_lat = acc_sc[...] / l_sc[...]
        for h in range(MLA_HEADS):
            o_ref[:, h * MLA_V:(h + 1) * MLA_V] = _dot(o_lat[h * t_new:(h + 1) * t_new].astype(BF16), wuv_ref[h])


def _decode(page_table, ql, qr, c_new, kr_new, ir_new, wuv_h, lat_pool, kr_pool, ir_pool, pages):
    bs, n_pages = page_table.shape
    t_new = c_new.shape[1]
    page = lat_pool.shape[1]

    def pool_spec(width, p):
        return pl.BlockSpec((None, page, width), lambda b, c, pt: (pt[b * n_pages + c * pages + p], 0, 0))

    per_b = lambda shape: pl.BlockSpec((None,) + shape, lambda b, c, pt: (b,) + (0,) * len(shape))
    in_specs = [pl.BlockSpec((MLA_HEADS, None, t_new, MLA_KV_RANK), lambda b, c, pt: (0, b, 0, 0)),
                pl.BlockSpec((MLA_HEADS, None, t_new, MLA_ROPE), lambda b, c, pt: (0, b, 0, 0)),
                per_b((t_new, MLA_KV_RANK)), per_b((t_new, MLA_ROPE)), per_b((t_new, 128)),
                pl.BlockSpec(wuv_h.shape, lambda b, c, pt: (0, 0, 0))]
    in_specs += [pool_spec(MLA_KV_RANK, p) for p in range(pages)]
    in_specs += [pool_spec(MLA_ROPE, p) for p in range(pages)]
    in_specs += [pool_spec(MLA_HEADS, p) for p in range(pages)]
    rows = MLA_HEADS * t_new
    grid_spec = pltpu.PrefetchScalarGridSpec(
        num_scalar_prefetch=1,
        grid=(bs, n_pages // pages),
        in_specs=in_specs,
        out_specs=pl.BlockSpec((None, t_new, MLA_HEADS * MLA_V), lambda b, c, pt: (b, 0, 0)),
        scratch_shapes=[pltpu.VMEM((rows, 1), F32), pltpu.VMEM((rows, 1), F32), pltpu.VMEM((rows, MLA_KV_RANK), F32)],
    )
    return pl.pallas_call(
        functools.partial(_decode_body, pages=pages, t_new=t_new),
        grid_spec=grid_spec,
        out_shape=jax.ShapeDtypeStruct((bs, t_new, MLA_HEADS * MLA_V), F32),
        compiler_params=_cparams("parallel", "arbitrary"),
        name="mla_sample_attn",
    )(page_table.reshape(-1), ql, qr, c_new, kr_new, ir_new, wuv_h,
      *([lat_pool] * pages), *([kr_pool] * pages), *([ir_pool] * pages))


def _gla_body(gq_ref, gk_ref, gv_ref, r_ref, alr_ref, wga_ref, bga_ref, gn_ref, *rest, chunk, sub, nb, carry):
    if carry:
        y_ref, s_out_ref = rest
        s_in_ref = s_out_ref

        @pl.when(pl.program_id(0) == 0)
        def _():
            s_out_ref[...] = jnp.zeros(s_out_ref.shape, F32)
    else:
        s_in_ref, y_ref, s_out_ref = rest
    row = lax.broadcasted_iota(jnp.int32, (chunk, chunk), 0)
    col = lax.broadcasted_iota(jnp.int32, (chunk, chunk), 1)
    tri = (col <= row).astype(F32)
    ones_cv = jnp.ones((chunk, GLA_DV), F32)
    brow = lax.broadcasted_iota(jnp.int32, (sub, chunk), 0)
    bcol = lax.broadcasted_iota(jnp.int32, (sub, chunk), 1)
    hi = lax.Precision.HIGHEST

    def one_batch(bi, carry_):
        alr = alr_ref[bi].astype(BF16)
        for h in range(GLA_HEADS):
            ks = slice(h * GLA_DK, (h + 1) * GLA_DK)
            vs = slice(h * GLA_DV, (h + 1) * GLA_DV)
            q = gq_ref[bi, :, ks] * GLA_DK ** -0.5
            k = gk_ref[bi, :, ks]
            v = gv_ref[bi, :, vs]
            vb = v.astype(BF16)
            a = _dot(alr, wga_ref[:, ks]) + bga_ref[:, ks]
            g = (jnp.minimum(a, 0.0) - jnp.log1p(jnp.exp(-jnp.abs(a)))) / GLA_TAU
            bc = _dot(tri, g, precision=hi)
            b_last = bc[chunk - 1:chunk]
            s0 = s_in_ref[bi, h]
            o = _dot((q * jnp.exp(bc)).astype(BF16), s0.astype(BF16))
            gn = gn_ref[...]
            for blk in range(chunk // sub):
                r0 = blk * sub
                qb = q[r0:r0 + sub]
                bb = bc[r0:r0 + sub]
                a_blk = jnp.zeros((sub, chunk), F32)
                for s in range(sub):
                    e = jnp.exp(jnp.minimum(bb - bb[s:s + 1], 0.0))
                    cs = jnp.sum(qb * e * k[r0 + s:r0 + s + 1], -1, keepdims=True)
                    a_blk = jnp.where(bcol == r0 + s, jnp.where(brow >= s, cs, 0.0), a_blk)
                if blk > 0:
                    b0 = bc[r0 - 1:r0]
                    qs = (qb * jnp.exp(bb - b0)).astype(BF16)
                    ksc = (k * jnp.exp(jnp.minimum(b0 - bc, 0.0))).astype(BF16)
                    a_blk = jnp.where(bcol < r0, _dot_nt(qs, ksc), a_blk)
                ob = o[r0:r0 + sub] + _dot(a_blk.astype(BF16), vb)
                rr = r_ref[bi, r0:r0 + sub, vs]
                y_ref[bi, r0:r0 + sub, vs] = _rms(ob, gn) * (rr * jax.nn.sigmoid(rr))
            dec = jnp.exp(_dot_tn(g, ones_cv, precision=hi))
            kd = (k * jnp.exp(b_last - bc)).astype(BF16)
            s_out_ref[bi, h] = dec * s0 + _dot_tn(kd, vb)
        return carry_

    lax.fori_loop(0, nb, one_batch, 0)


def _gla(z3, state, wga, bga, gn, chunk, sub, nb):
    b, s, _ = z3.shape
    carry = state is None
    if carry:
        grid = (s // chunk,)
        zmap = lambda blk: (lambda c: (0, c, blk))
        smap = lambda c: (0, 0, 0, 0)
        nb = b
    else:
        grid = (b // nb,)
        zmap = lambda blk: (lambda i: (i, 0, blk))
        smap = lambda i: (i, 0, 0, 0)
    zspec = lambda wb: pl.BlockSpec((nb, chunk, wb[0]), zmap(wb[1]))
    const = lambda shape: pl.BlockSpec(shape, lambda i: (0,) * len(shape))
    sspec = pl.BlockSpec((nb, GLA_HEADS, GLA_DK, GLA_DV), smap)
    in_specs = [zspec(ZB_GQ), zspec(ZB_GK), zspec(ZB_GV), zspec(ZB_R), zspec(ZB_ALR),
                const(wga.shape), const(bga.shape), const(gn.shape)]
    args = [z3, z3, z3, z3, z3, wga, bga, gn]
    if not carry:
        in_specs.append(sspec)
        args.append(state)
    return pl.pallas_call(
        functools.partial(_gla_body, chunk=chunk, sub=sub, nb=nb, carry=carry),
        grid=grid,
        in_specs=in_specs,
        out_specs=[pl.BlockSpec((nb, chunk, GLA_HEADS * GLA_DV), zmap(0)), sspec],
        out_shape=[jax.ShapeDtypeStruct((b, s, GLA_HEADS * GLA_DV), F32),
                   jax.ShapeDtypeStruct((b, GLA_HEADS, GLA_DK, GLA_DV), F32)],
        compiler_params=_cparams("arbitrary"),
        name="gla_prompt" if carry else "gla_sample",
    )(*args)


def _out_proj_body(mla_ref, gla_ref, x_ref, nm_ref, w_ref, o_ref, *, mla_feature_major):
    mla = mla_ref[...].T if mla_feature_major else mla_ref[...]
    half = mla.shape[1]
    mn = _rms(mla, nm_ref[...]).astype(BF16)
    o_ref[...] = x_ref[...] + _dot(mn, w_ref[:half]) + _dot(gla_ref[...].astype(BF16), w_ref[half:])


def _out_proj(mla_o, gla_y, x, nm, w, tm):
    n, d = x.shape
    half = gla_y.shape[1]
    feature_major = mla_o.ndim == 3
    if feature_major:
        per_b = mla_o.shape[2] // tm
        mla_spec = pl.BlockSpec((None, half, tm), lambda i: (i // per_b, 0, lax.rem(i, per_b)))
    else:
        mla_spec = pl.BlockSpec((tm, half), lambda i: (i, 0))
    return pl.pallas_call(
        functools.partial(_out_proj_body, mla_feature_major=feature_major),
        grid=(n // tm,),
        in_specs=[mla_spec, pl.BlockSpec((tm, half), lambda i: (i, 0)),
                  pl.BlockSpec((tm, d), lambda i: (i, 0)), pl.BlockSpec((1, half), lambda i: (0, 0)),
                  pl.BlockSpec(w.shape, lambda i: (0, 0))],
        out_specs=pl.BlockSpec((tm, d), lambda i: (i, 0)),
        out_shape=jax.ShapeDtypeStruct((n, d), F32),
        compiler_params=_cparams("parallel"),
        name="out_proj",
    )(mla_o, gla_y, x, nm.reshape(1, half), w)


def _top_desc(s, n):
    vals = []
    w = s
    for _ in range(n):
        m = jnp.max(w, axis=0, keepdims=True)
        vals.append(m)
        w = jnp.where(w == m, NEG_INF, w)
    return vals


def _peer_route_body(h_ref, g_ref, wqt_ref, k1_ref, k2_ref, xnt_ref, s2_ref, e2_ref, thr_ref, e1z_ref):
    xnt = _rms(h_ref[...], g_ref[...]).T.astype(BF16)
    xnt_ref[...] = xnt
    qt = _dot(wqt_ref[...], xnt).astype(BF16)
    half = PEER_N_KEYS
    for h in range(PEER_HEADS):
        s1 = _dot(k1_ref[...], qt[(2 * h) * half:(2 * h + 1) * half])
        s2 = _dot(k2_ref[...], qt[(2 * h + 1) * half:(2 * h + 2) * half])
        v1 = _top_desc(s1, PEER_TOPK)
        v2 = _top_desc(s2, PEER_TOPK)
        pairs = [(a, b) for a in range(PEER_TOPK) for b in range(PEER_TOPK // (a + 1))]
        sums = {ab: v1[ab[0]] + v2[ab[1]] for ab in pairs}
        npad = -len(pairs) % 8
        stack = jnp.concatenate([sums[ab] for ab in pairs] + [jnp.full_like(v1[0], NEG_INF)] * npad, 0)
        tau = _top_desc(stack, PEER_TOPK)[-1]
        top = v1[0] + v2[0]
        zsum = jnp.sum(jnp.where(stack >= tau, jnp.exp(stack - top), 0.0), axis=0, keepdims=True)
        thr = jnp.full(s1.shape, POS_INF, F32)
        for a in range(PEER_TOPK):
            ta = jnp.full_like(tau, POS_INF)
            for b in range(PEER_TOPK // (a + 1)):
                ta = jnp.minimum(ta, jnp.where(sums[(a, b)] >= tau, v2[b], POS_INF))
            thr = jnp.where(s1 == v1[a], ta, thr)
        e2 = jnp.exp(s2 - v2[0])
        e1z = jnp.exp(s1 - v1[0]) / zsum
        for lg in range(thr.shape[1] // 128):
            lanes = slice(lg * 128, (lg + 1) * 128)
            s2_ref[h, lg] = s2[:, lanes]
            e2_ref[h, lg] = e2[:, lanes]
            thr_ref[h, lg] = thr[:, lanes]
            e1z_ref[h, lg] = e1z[:, lanes]


def _peer_route(h, g, wqt, k1, k2, tt):
    n, d = h.shape
    per_group = jax.ShapeDtypeStruct((PEER_HEADS, n // 128, PEER_N_KEYS, 128), F32)
    gspec = pl.BlockSpec((PEER_HEADS, tt // 128, PEER_N_KEYS, 128), lambda i: (0, i, 0, 0))
    return pl.pallas_call(
        _peer_route_body,
        grid=(n // tt,),
        in_specs=[pl.BlockSpec((tt, d), lambda i: (i, 0)), pl.BlockSpec((1, d), lambda i: (0, 0)),
                  pl.BlockSpec(wqt.shape, lambda i: (0, 0)), pl.BlockSpec(k1.shape, lambda i: (0, 0)),
                  pl.BlockSpec(k2.shape, lambda i: (0, 0))],
        out_specs=[pl.BlockSpec((d, tt), lambda i: (0, i)), gspec, gspec, gspec, gspec],
        out_shape=[jax.ShapeDtypeStruct((d, n), BF16), per_group, per_group, per_group, per_group],
        compiler_params=_cparams("parallel"),
        name="peer_route",
    )(h, g.reshape(1, d), wqt, k1, k2)


def _peer_dense_body(xnt_ref, u_ref, vt_ref, s2_ref, e2_ref, thr_ref, e1z_ref, yt_ref, act_sc, ca_sc):
    @pl.when(pl.program_id(1) == 0)
    def _():
        yt_ref[...] = jnp.zeros(yt_ref.shape, F32)

    n_lg, te, _ = act_sc.shape
    act = _dot(u_ref[...], xnt_ref[...])
    act = 0.5 * act * (1.0 + lax.erf(act * (2.0 ** -0.5)))
    for lg in range(n_lg):
        act_sc[lg] = act[:, lg * 128:(lg + 1) * 128]
    groups = te // PEER_N_KEYS
    n_sub = PEER_N_KEYS // PEER_SUB

    def gate_block(blk, carry):
        lg = blk // n_sub
        k0 = pl.multiple_of(lax.rem(blk, n_sub) * PEER_SUB, PEER_SUB)
        coef = [jnp.zeros((PEER_SUB, 128), F32)] * groups
        for h in range(PEER_HEADS):
            s2b = s2_ref[h, lg, pl.ds(k0, PEER_SUB), :]
            e2b = e2_ref[h, lg, pl.ds(k0, PEER_SUB), :]
            for gi in range(groups):
                thr = jnp.broadcast_to(thr_ref[h, lg, gi:gi + 1, :], (PEER_SUB, 128))
                e1z = jnp.broadcast_to(e1z_ref[h, lg, gi:gi + 1, :], (PEER_SUB, 128))
                coef[gi] = coef[gi] + jnp.where(s2b >= thr, e2b, 0.0) * e1z
        for gi in range(groups):
            rows = pl.ds(pl.multiple_of(gi * PEER_N_KEYS + k0, PEER_SUB), PEER_SUB)
            ca_sc[lg, rows, :] = (coef[gi] * act_sc[lg, rows, :]).astype(BF16)
        return carry

    lax.fori_loop(0, n_lg * n_sub, gate_block, 0)
    ca = jnp.concatenate([ca_sc[lg] for lg in range(n_lg)], 1)
    yt_ref[...] += _dot(vt_ref[...], ca)


def _peer_dense(xnt, u, vt, s2, e2, thr, e1z, tt, te):
    d, n = xnt.shape
    ne = u.shape[0] // te
    gpt = te // PEER_N_KEYS
    full = pl.BlockSpec((PEER_HEADS, tt // 128, PEER_N_KEYS, 128), lambda i, j: (0, i, 0, 0))
    part = pl.BlockSpec((PEER_HEADS, tt // 128, gpt, 128), lambda i, j: (0, i, j, 0))
    return pl.pallas_call(
        _peer_dense_body,
        grid=(n // tt, ne),
        in_specs=[pl.BlockSpec((d, tt), lambda i, j: (0, i)), pl.BlockSpec((te, d), lambda i, j: (j, 0)),
                  pl.BlockSpec((d, te), lambda i, j: (0, j)), full, full, part, part],
        out_specs=pl.BlockSpec((d, tt), lambda i, j: (0, i)),
        out_shape=jax.ShapeDtypeStruct((d, n), F32),
        scratch_shapes=[pltpu.VMEM((tt // 128, te, 128), F32), pltpu.VMEM((tt // 128, te, 128), BF16)],
        compiler_params=_cparams("parallel", "arbitrary"),
        name="peer_dense",
    )(xnt, u, vt, s2, e2, thr, e1z)


def _ple_body(h_ref, yt_ref, p_ref, g_ref, wg_ref, wp_ref, o_ref):
    h2 = h_ref[...] + yt_ref[...].T
    gate = jax.nn.sigmoid(_dot(_rms(h2, g_ref[...]).astype(BF16), wg_ref[...]))
    o_ref[...] = h2 + gate * _dot(p_ref[...].astype(BF16), wp_ref[...])


def _ple(h, yt, p, g, wg, wp, tm):
    n, d = h.shape
    return pl.pallas_call(
        _ple_body,
        grid=(n // tm,),
        in_specs=[pl.BlockSpec((tm, d), lambda i: (i, 0)), pl.BlockSpec((d, tm), lambda i: (0, i)),
                  pl.BlockSpec((tm, p.shape[1]), lambda i: (i, 0)), pl.BlockSpec((1, d), lambda i: (0, 0)),
                  pl.BlockSpec(wg.shape, lambda i: (0, 0)), pl.BlockSpec(wp.shape, lambda i: (0, 0))],
        out_specs=pl.BlockSpec((tm, d), lambda i: (i, 0)),
        out_shape=jax.ShapeDtypeStruct((n, d), F32),
        compiler_params=_cparams("parallel"),
        name="ple",
    )(h, yt, p, g.reshape(1, d), wg, wp)


def _rope_table(pos):
    half = MLA_ROPE // 2
    inv_freq = ROPE_THETA ** (-jnp.arange(half, dtype=F32) / half)
    ang = pos.astype(F32)[:, None] * inv_freq
    cos, sin = jnp.cos(ang), jnp.sin(ang)
    return jnp.concatenate([cos, cos, sin, sin], -1)


def _tile(n, t):
    return t if n % t == 0 else n


def _finish(x2, mla_o, gla_y, p2, w):
    n = x2.shape[0]
    h = _out_proj(mla_o, gla_y, x2, w["norm_mla_out"], w["w_out"], _tile(n, TM_OUT))
    xnt, s2, e2, thr, e1z = _peer_route(h, w["norm_ffn"], w["wqt"], w["k1"], w["k2"], _tile(n, TT_ROUTE))
    yt = _peer_dense(xnt, w["u"], w["vt"], s2, e2, thr, e1z, _tile(n, TT_DENSE), TE_DENSE)
    return _ple(h, yt, p2, w["norm_ple"], w["w_ple_gate"], w["w_ple_proj"], _tile(n, TM_PLE))


def kernel(x_prompt, x_sample, cache_kv_latent, cache_k_rope, cache_k_inv_rms, state_gla, page_table, p_prompt,
           p_sample, norm_mix, w_in, norm_q_lat, w_uq, norm_kv_lat, w_uk, w_uv, qk_gain_q, qk_gain_k, norm_mla_out,
           w_gla_a, b_gla_a, norm_gla_out, w_out, norm_ffn, peer_w_q, peer_keys1, peer_keys2, peer_u, peer_v,
           norm_ple, w_ple_gate, w_ple_proj):
    depth = w_in.shape[0]
    b, s, d = x_prompt.shape
    bs, t_new, _ = x_sample.shape
    n_pages = page_table.shape[1]
    page = cache_kv_latent.shape[2]
    past_len = n_pages * page
    tab_p = _rope_table(jnp.arange(s))
    tab_s = _rope_table(past_len + jnp.arange(t_new))
    hp, hs = x_prompt.reshape(b * s, d), x_sample.reshape(bs * t_new, d)
    outs = [[] for _ in range(8)]
    for i in range(depth):
        wi = w_in[i]
        cuts = [0, 512, 768, 832, 1344, 1856, 2880, 3904, 3920]
        cq_w, ckv_w, kr_w, gq_w, gk_w, gv_w, r_w, alr_w = [wi[:, cuts[j]:cuts[j + 1]] for j in range(8)]
        w_in_ext = jnp.concatenate(
            [cq_w, gq_w, gk_w, ckv_w, kr_w, _rot_cols(kr_w), alr_w,
             jnp.zeros((d, 128 - GLA_GATE_RANK), F32), gv_w, r_w], 1).astype(BF16)
        wuq3 = w_uq[i].reshape(MLA_Q_RANK, MLA_HEADS, MLA_QK)
        wuq_ext = jnp.concatenate([wuq3, _rot_cols(wuq3[..., MLA_NOPE:])], -1).reshape(MLA_Q_RANK, -1).astype(BF16)
        gq, gk = qk_gain_q[i], qk_gain_k[i]
        gq_ext = jnp.concatenate([gq, _rot_gain(gq[MLA_NOPE:])]).reshape(1, -1)
        gk_ext = jnp.concatenate([gk, _rot_gain(gk[MLA_NOPE:])]).reshape(1, -1)
        wuk = w_uk[i].reshape(MLA_KV_RANK, -1).astype(BF16)
        wuv = w_uv[i].reshape(MLA_KV_RANK, -1).astype(BF16)
        wuk_t = jnp.transpose(w_uk[i], (1, 2, 0)).astype(BF16)
        wuv_h = jnp.transpose(w_uv[i], (1, 0, 2)).astype(BF16)
        wga = jnp.concatenate([w_gla_a[i], jnp.zeros((128 - GLA_GATE_RANK, w_gla_a.shape[2]), F32)], 0).astype(BF16)
        bga = b_gla_a[i].reshape(1, -1)
        gn = norm_gla_out[i].reshape(1, -1)
        w = {"norm_mla_out": norm_mla_out[i], "w_out": w_out[i].astype(BF16), "norm_ffn": norm_ffn[i],
             "wqt": peer_w_q[i].T.astype(BF16), "k1": peer_keys1[i].astype(BF16), "k2": peer_keys2[i].astype(BF16),
             "u": peer_u[i].astype(BF16), "vt": peer_v[i].T.astype(BF16), "norm_ple": norm_ple[i],
             "w_ple_gate": w_ple_gate[i].astype(BF16), "w_ple_proj": w_ple_proj[i].astype(BF16)}
        prep = (norm_q_lat[i], norm_kv_lat[i], gq_ext, gk_ext, wuq_ext, wuk)

        n = b * s
        z = _norm_matmul(hp, norm_mix[i], w_in_ext, _tile(n, TM_IN), TN_IN)
        tm = _tile(s, TM_PREP)
        qt, kt, vt, lat, kr, ir = _mla_prep(z, tab_p, s // tm, *prep, wuv, tm, sample=False)
        tq = _tile(s, FLASH_TILE)
        mla_o = _flash(qt.reshape(b, s, -1), kt.reshape(b, s, -1), vt, tq, FLASH_HEADS)
        gla_y, s_fin = _gla(z.reshape(b, s, Z_COLS), None, wga, bga, gn, GLA_CHUNK, GLA_SUB, b)
        hp_next = _finish(hp, mla_o, gla_y.reshape(n, -1), p_prompt[i].reshape(n, -1), w)
        outs[0].append(lat.reshape(b, s, -1))
        outs[1].append(kr.reshape(b, s, -1))
        outs[2].append(ir[:, :MLA_HEADS].reshape(b, s, -1))
        outs[3].append(s_fin)

        n = bs * t_new
        z = _norm_matmul(hs, norm_mix[i], w_in_ext, _tile(n, TM_IN), TN_IN)
        tm = _tile(n, TM_PREP)
        tab = jnp.tile(tab_s, (tm // t_new, 1))
        ql, qr, lat, kr, ir = _mla_prep(z, tab, 1, *prep, wuk_t, tm, sample=True)
        mla_o = _decode(page_table, ql.reshape(MLA_HEADS, bs, t_new, -1), qr.reshape(MLA_HEADS, bs, t_new, -1),
                        lat.reshape(bs, t_new, -1), kr.reshape(bs, t_new, -1), ir.reshape(bs, t_new, -1), wuv_h,
                        cache_kv_latent[i], cache_k_rope[i], cache_k_inv_rms[i], _tile(n_pages, DECODE_PAGES))
        gla_y, s_new = _gla(z.reshape(bs, t_new, Z_COLS), state_gla[i], wga, bga, gn, t_new, t_new,
                            _tile(bs, GLA_SAMPLE_ROWS))
        hs_next = _finish(hs, mla_o.reshape(n, -1), gla_y.reshape(n, -1), p_sample[i].reshape(n, -1), w)
        outs[4].append(lat.reshape(bs, t_new, -1))
        outs[5].append(kr.reshape(bs, t_new, -1))
        outs[6].append(ir[:, :MLA_HEADS].reshape(bs, t_new, -1))
        outs[7].append(s_new)
        hp, hs = hp_next, hs_next
    return (hp.reshape(b, s, d), hs.reshape(bs, t_new, d)) + tuple(jnp.stack(o, 0) for o in outs)
```

```python
import functools

import jax
import jax.numpy as jnp
from jax import lax
from jax.experimental import pallas as pl
from jax.experimental.pallas import tpu as pltpu

F32 = jnp.float32
BF16 = jnp.bfloat16
EPS = 1e-6
NEG_INF = float("-inf")
POS_INF = float("inf")

MLA_HEADS = 8
MLA_NOPE = 128
MLA_ROPE = 64
MLA_QK = MLA_NOPE + MLA_ROPE
MLA_V = 128
MLA_Q_RANK = 512
MLA_KV_RANK = 256
MLA_SCALE = MLA_QK ** -0.5
LOG2E = 1.4426950408889634
MLA_HEAD_PAD = 256
ROPE_THETA = 10000.0
GLA_HEADS = 4
GLA_DK = 128
GLA_DV = 256
GLA_GATE_RANK = 16
GLA_TAU = 16.0
GLA_CHUNK = 64
GLA_SUB = 16
PEER_HEADS = 8
PEER_N_KEYS = 128
PEER_TOPK = 16
PEER_SUB = 32
PLE_DIM = 256

Z_COLS = 4096
ZB_CQ = (512, 0)
ZB_GQ = (512, 1)
ZB_GK = (512, 2)
ZB_CKV = (256, 6)
ZB_KR = (128, 14)
ZB_ALR = (128, 15)
ZB_GV = (1024, 2)
ZB_R = (1024, 3)

V7X_VMEM_LIMIT = 56 * 1024 * 1024

TM_IN, TN_IN = 512, 1024
TM_PREP = 256
FLASH_TILE = 512
FLASH_HEADS = 8
FLASH_SUB = 256
DECODE_PAGES = 32
GLA_SAMPLE_ROWS = 8
GLA_ROWS_PER_ITER = 2
TM_OUT = 256
TT_ROUTE = 256
TT_DENSE, TE_DENSE = 512, 1024
TM_PLE = 256


def _cparams(*sem):
    return pltpu.CompilerParams(dimension_semantics=sem, vmem_limit_bytes=V7X_VMEM_LIMIT)


def _dot(a, b, **kw):
    return jnp.dot(a, b, preferred_element_type=F32, **kw)


def _dot_nt(a, b):
    return lax.dot_general(a, b, (((1,), (1,)), ((), ())), preferred_element_type=F32)


def _dot_tn(a, b, **kw):
    return lax.dot_general(a, b, (((0,), (0,)), ((), ())), preferred_element_type=F32, **kw)


def _rms(x, g):
    return x * lax.rsqrt(jnp.mean(x * x, -1, keepdims=True) + EPS) * g


def _rot_cols(w):
    h = w.shape[-1] // 2
    return jnp.concatenate([-w[..., h:], w[..., :h]], -1)


def _rot_gain(g):
    h = g.shape[-1] // 2
    return jnp.concatenate([g[..., h:], g[..., :h]], -1)


def _norm_matmul_body(x_ref, g_ref, w_ref, o_ref, xn_sc):
    @pl.when(pl.program_id(1) == 0)
    def _():
        xn_sc[...] = _rms(x_ref[...], g_ref[...]).astype(BF16)

    o_ref[...] = _dot(xn_sc[...], w_ref[...])


def _norm_matmul(x, g, w, tm, tn):
    n, d = x.shape
    nc = w.shape[1]
    return pl.pallas_call(
        _norm_matmul_body,
        grid=(n // tm, nc // tn),
        in_specs=[pl.BlockSpec((tm, d), lambda i, j: (i, 0)),
                  pl.BlockSpec((1, d), lambda i, j: (0, 0)),
                  pl.BlockSpec((d, tn), lambda i, j: (0, j))],
        out_specs=pl.BlockSpec((tm, tn), lambda i, j: (i, j)),
        out_shape=jax.ShapeDtypeStruct((n, nc), F32),
        scratch_shapes=[pltpu.VMEM((tm, d), BF16)],
        compiler_params=_cparams("parallel", "arbitrary"),
        name="in_proj",
    )(x, g.reshape(1, d), w)


def _mla_prep_body(cq_ref, ckv_ref, krb_ref, tab_ref, nq_ref, nkv_ref, gq_ref, gk_ref, wuq_ref, wuk_ref, wx_ref,
                   *outs, sample):
    if sample:
        ql_ref, qr_ref, lat_ref, kr_ref, ir_ref = outs
    else:
        qt_ref, kt_ref, vt_ref, lat_ref, kr_ref, ir_ref = outs
    tab = tab_ref[...]
    gq = gq_ref[...]
    gk = gk_ref[...]
    cqn = _rms(cq_ref[...], nq_ref[...]).astype(BF16)
    qraw = _dot(cqn, wuq_ref[...])
    c = _rms(ckv_ref[...], nkv_ref[...])
    lat_ref[...] = c
    cb = c.astype(BF16)
    kn = _dot(cb, wuk_ref[...])
    if not sample:
        vt_ref[...] = _dot(cb, wx_ref[...]).T.astype(BF16)
    krb = krb_ref[...]
    lane = lax.broadcasted_iota(jnp.int32, krb.shape, 1)
    ss_kr = jnp.sum(jnp.where(lane < MLA_ROPE, krb * krb, 0.0), -1, keepdims=True)
    t = krb * gk[:, MLA_NOPE:] * tab
    kr2 = t + pltpu.roll(t, MLA_ROPE, 1)
    kr_ref[...] = kr2[:, :MLA_ROPE]
    krz = jnp.where(lane < MLA_ROPE, kr2, 0.0)
    lane_q = lax.broadcasted_iota(jnp.int32, (krb.shape[0], MLA_HEAD_PAD), 1)
    ir_acc = jnp.zeros(krb.shape, F32)
    for h in range(MLA_HEADS):
        knh = kn[:, h * MLA_NOPE:(h + 1) * MLA_NOPE]
        ss = jnp.sum(knh * knh, -1, keepdims=True) + ss_kr
        inv = lax.rsqrt(ss / MLA_QK + EPS)
        ir_acc = jnp.where(lane == h, inv, ir_acc)
        qh = qraw[:, h * MLA_HEAD_PAD:(h + 1) * MLA_HEAD_PAD]
        ssq = jnp.sum(jnp.where(lane_q < MLA_QK, qh * qh, 0.0), -1, keepdims=True)
        qinv = lax.rsqrt(ssq / MLA_QK + EPS)
        qn = qh[:, :MLA_NOPE] * qinv * gq[:, :MLA_NOPE]
        tq = qh[:, MLA_NOPE:] * qinv * gq[:, MLA_NOPE:] * tab
        qr2 = tq + pltpu.roll(tq, MLA_ROPE, 1)
        if sample:
            qg = (qn * gk[:, :MLA_NOPE]).astype(BF16)
            ql_ref[h] = _dot(qg, wx_ref[h]) * MLA_SCALE
            qr_ref[h] = qr2[:, :MLA_ROPE] * MLA_SCALE
        else:
            lo = h * MLA_HEAD_PAD
            qt_ref[:, lo:lo + MLA_NOPE] = (qn * (MLA_SCALE * LOG2E)).astype(BF16)
            qt_ref[:, lo + MLA_NOPE:lo + MLA_HEAD_PAD] = (qr2 * (MLA_SCALE * LOG2E)).astype(BF16)
            kt_ref[:, lo:lo + MLA_NOPE] = (knh * gk[:, :MLA_NOPE] * inv).astype(BF16)
            kt_ref[:, lo + MLA_NOPE:lo + MLA_HEAD_PAD] = (krz * inv).astype(BF16)
    ir_ref[...] = ir_acc


def _mla_prep(z, tab, tab_blocks, nq, nkv, gq_ext, gk_ext, wuq_ext, wuk, wx, tm, sample):
    n = z.shape[0]
    hp = MLA_HEADS * MLA_HEAD_PAD

    def const(shape):
        return pl.BlockSpec(shape, lambda i: (0,) * len(shape))

    in_specs = [pl.BlockSpec((tm, ZB_CQ[0]), lambda i: (i, ZB_CQ[1])),
                pl.BlockSpec((tm, ZB_CKV[0]), lambda i: (i, ZB_CKV[1])),
                pl.BlockSpec((tm, ZB_KR[0]), lambda i: (i, ZB_KR[1])),
                pl.BlockSpec((tm, 128), lambda i: (i % tab_blocks, 0)),
                const((1, MLA_Q_RANK)), const((1, MLA_KV_RANK)), const((1, MLA_HEAD_PAD)), const((1, MLA_HEAD_PAD)),
                const(wuq_ext.shape), const(wuk.shape), const(wx.shape)]
    row = lambda w: pl.BlockSpec((tm, w), lambda i: (i, 0))
    tail_specs = [row(MLA_KV_RANK), row(MLA_ROPE), row(128)]
    tail_shapes = [jax.ShapeDtypeStruct((n, MLA_KV_RANK), F32), jax.ShapeDtypeStruct((n, MLA_ROPE), F32),
                   jax.ShapeDtypeStruct((n, 128), F32)]
    if sample:
        head = lambda w: pl.BlockSpec((MLA_HEADS, tm, w), lambda i: (0, i, 0))
        out_specs = [head(MLA_KV_RANK), head(MLA_ROPE)] + tail_specs
        out_shape = [jax.ShapeDtypeStruct((MLA_HEADS, n, MLA_KV_RANK), F32),
                     jax.ShapeDtypeStruct((MLA_HEADS, n, MLA_ROPE), F32)] + tail_shapes
    else:
        out_specs = [row(hp), row(hp), pl.BlockSpec((MLA_HEADS * MLA_V, tm), lambda i: (0, i))] + tail_specs
        out_shape = [jax.ShapeDtypeStruct((n, hp), BF16), jax.ShapeDtypeStruct((n, hp), BF16),
                     jax.ShapeDtypeStruct((MLA_HEADS * MLA_V, n), BF16)] + tail_shapes
    return pl.pallas_call(
        functools.partial(_mla_prep_body, sample=sample),
        grid=(n // tm,),
        in_specs=in_specs, out_specs=out_specs, out_shape=out_shape,
        compiler_params=_cparams("parallel"),
        name="mla_prep_sample" if sample else "mla_prep_prompt",
    )(z, z, z, tab, nq.reshape(1, -1), nkv.reshape(1, -1), gq_ext, gk_ext, wuq_ext, wuk, wx)


def _flash_body(q_ref, k_ref, vt_ref, o_ref, m_sc, l_sc, acc_sc, *, heads, sub):
    i = pl.program_id(2)
    j = pl.program_id(3)

    @pl.when(j == 0)
    def _():
        m_sc[...] = jnp.full(m_sc.shape, NEG_INF, F32)
        l_sc[...] = jnp.zeros(l_sc.shape, F32)
        acc_sc[...] = jnp.zeros(acc_sc.shape, F32)

    tile = q_ref.shape[0]

    def step(diagonal):
        def q_cols(c, carry):
            r0 = pl.multiple_of(c * sub, sub)
            cols = pl.ds(r0, sub)
            vrows = [slice(h * MLA_V, (h + 1) * MLA_V) for h in range(heads)]
            prev = [(m_sc[h, :, cols], l_sc[h, :, cols], acc_sc[vrows[h], cols]) for h in range(heads)]
            new = []

            def scores(h):
                qk = slice(h * MLA_HEAD_PAD, (h + 1) * MLA_HEAD_PAD)
                return _dot_nt(k_ref[:, qk], q_ref[cols, qk])

            ahead = 2
            pending = [scores(h) for h in range(min(ahead, heads))]
            for h in range(heads):
                s = pending.pop(0)
                if diagonal:
                    keep = (lax.broadcasted_iota(jnp.int32, s.shape, 0)
                            <= r0 + lax.broadcasted_iota(jnp.int32, s.shape, 1))
                    s = jnp.where(keep, s, NEG_INF)
                m_prev, l_prev, acc_prev = prev[h]
                m_new = jnp.maximum(m_prev, jnp.max(s, 0, keepdims=True))
                alpha = jnp.exp2(m_prev - m_new)
                p = jnp.exp2(s - m_new)
                l_new = alpha * l_prev + jnp.sum(p, 0, keepdims=True)
                acc = alpha * acc_prev + _dot(vt_ref[vrows[h], :], p.astype(BF16))
                new.append((m_new, l_new, acc))
                if h + ahead < heads:
                    pending.append(scores(h + ahead))
            for h, (m_new, l_new, acc) in enumerate(new):
                if diagonal:
                    o_ref[vrows[h], cols] = acc / l_new
                else:
                    m_sc[h, :, cols] = m_new
                    l_sc[h, :, cols] = l_new
                    acc_sc[vrows[h], cols] = acc
            return carry

        lax.fori_loop(0, tile // sub, q_cols, 0)

    pl.when(j < i)(lambda: step(False))
    pl.when(j == i)(lambda: step(True))


def _flash(qt, kt, vt, tile, heads):
    b, s, _ = qt.shape
    nt = s // tile
    return pl.pallas_call(
        functools.partial(_flash_body, heads=heads, sub=min(FLASH_SUB, tile)),
        grid=(b, MLA_HEADS // heads, nt, nt),
        in_specs=[pl.BlockSpec((None, tile, heads * MLA_HEAD_PAD), lambda bb, h, i, j: (bb, i, h)),
                  pl.BlockSpec((None, tile, heads * MLA_HEAD_PAD), lambda bb, h, i, j: (bb, jnp.minimum(j, i), h)),
                  pl.BlockSpec((heads * MLA_V, tile), lambda bb, h, i, j: (h, bb * nt + jnp.minimum(j, i)))],
        out_specs=pl.BlockSpec((None, heads * MLA_V, tile), lambda bb, h, i, j: (bb, h, i)),
        out_shape=jax.ShapeDtypeStruct((b, MLA_HEADS * MLA_V, s), F32),
        scratch_shapes=[pltpu.VMEM((heads, 1, tile), F32), pltpu.VMEM((heads, 1, tile), F32),
                        pltpu.VMEM((heads * MLA_V, tile), F32)],
        compiler_params=_cparams("parallel", "parallel", "parallel", "arbitrary"),
        name="mla_prompt_attn",
    )(qt, kt, vt)


def _decode_body(pt_ref, ql_ref, qr_ref, cn_ref, krn_ref, irn_ref, wuv_ref, lat_pool, krt_pool, irt_pool,
                 o_ref, m_sc, l_sc, acc_sc, lat_buf, krt_buf, irt_buf, sems, *, pages, n_chunks, t_new):
    rows = MLA_HEADS * t_new
    page = lat_buf.shape[1] // pages
    t = pl.program_id(0)
    last = pl.num_programs(0) - 1
    c = lax.rem(t, n_chunks)
    slot = lax.rem(t, 2)

    def chunk_copies(step, buf_slot):
        copies = []
        for p in range(pages):
            pg = pt_ref[step * pages + p]
            keys = pl.ds(p * page, page)
            copies.append(pltpu.make_async_copy(lat_pool.at[pg], lat_buf.at[buf_slot, keys, :], sems.at[buf_slot, 0]))
            copies.append(pltpu.make_async_copy(krt_pool.at[pg], krt_buf.at[buf_slot, :, keys], sems.at[buf_slot, 1]))
            copies.append(pltpu.make_async_copy(irt_pool.at[pg], irt_buf.at[buf_slot, :, keys], sems.at[buf_slot, 2]))
        return copies

    @pl.when(t == 0)
    def _():
        for cp in chunk_copies(0, 0):
            cp.start()

    @pl.when(c == 0)
    def _():
        m_sc[...] = jnp.full(m_sc.shape, NEG_INF, F32)
        l_sc[...] = jnp.zeros(l_sc.shape, F32)
        acc_sc[...] = jnp.zeros(acc_sc.shape, F32)

    for cp in chunk_copies(t, slot):
        cp.wait()
    nxt = jnp.minimum(t + 1, last)
    for cp in chunk_copies(nxt, 1 - slot):
        cp.start()

    ql = ql_ref[...].reshape(rows, MLA_KV_RANK).astype(BF16)
    qr = qr_ref[...].reshape(rows, MLA_ROPE).astype(BF16)

    def attend(lat_b, krt_b, irt, mask=None):
        s = _dot_nt(ql, lat_b) + _dot(qr, krt_b)
        s = (s.reshape(MLA_HEADS, t_new, -1) * irt[:, None, :]).reshape(rows, -1)
        if mask is not None:
            s = jnp.where(mask, s, NEG_INF)
        m_prev = m_sc[...]
        m_new = jnp.maximum(m_prev, jnp.max(s, -1, keepdims=True))
        alpha = jnp.exp(m_prev - m_new)
        p = jnp.exp(s - m_new)
        l_sc[...] = alpha * l_sc[...] + jnp.sum(p, -1, keepdims=True)
        acc_sc[...] = alpha * acc_sc[...] + _dot(p.astype(BF16), lat_b)
        m_sc[...] = m_new

    attend(lat_buf[slot].astype(BF16), krt_buf[slot].astype(BF16), irt_buf[slot])

    @pl.when(c == n_chunks - 1)
    def _():
        pad = 128 - t_new
        lb = jnp.concatenate([cn_ref[...], jnp.zeros((pad, MLA_KV_RANK), F32)], 0).astype(BF16)
        kb = jnp.concatenate([krn_ref[...], jnp.zeros((pad, MLA_ROPE), F32)], 0).T.astype(BF16)
        ib = jnp.concatenate([irn_ref[...], jnp.zeros((pad, 128), F32)], 0).T[:MLA_HEADS]
        tq = lax.rem(lax.broadcasted_iota(jnp.int32, (rows, 128), 0), t_new)
        key = lax.broadcasted_iota(jnp.int32, (rows, 128), 1)
        attend(lb, kb, ib, mask=key <= tq)
        o_lat = acc_sc[...] / l_sc[...]
        for h in range(MLA_HEADS):
            o_ref[:, h * MLA_V:(h + 1) * MLA_V] = _dot(o_lat[h * t_new:(h + 1) * t_new].astype(BF16), wuv_ref[h])

    @pl.when(t == last)
    def _():
        for cp in chunk_copies(last, 1 - slot):
            cp.wait()


def _decode(page_table, ql, qr, c_new, kr_new, ir_new, wuv_h, lat_pool, krt_pool, irt_pool, pages):
    bs, n_pages = page_table.shape
    t_new = c_new.shape[1]
    page = lat_pool.shape[1]
    n_chunks = n_pages // pages
    keys = pages * page
    rows = MLA_HEADS * t_new
    per_b = lambda shape: pl.BlockSpec((None,) + shape, lambda t, pt: (t // n_chunks,) + (0,) * len(shape))
    hbm = pl.BlockSpec(memory_space=pl.ANY)
    in_specs = [pl.BlockSpec((MLA_HEADS, None, t_new, MLA_KV_RANK), lambda t, pt: (0, t // n_chunks, 0, 0)),
                pl.BlockSpec((MLA_HEADS, None, t_new, MLA_ROPE), lambda t, pt: (0, t // n_chunks, 0, 0)),
                per_b((t_new, MLA_KV_RANK)), per_b((t_new, MLA_ROPE)), per_b((t_new, 128)),
                pl.BlockSpec(wuv_h.shape, lambda t, pt: (0, 0, 0)), hbm, hbm, hbm]
    grid_spec = pltpu.PrefetchScalarGridSpec(
        num_scalar_prefetch=1,
        grid=(bs * n_chunks,),
        in_specs=in_specs,
        out_specs=per_b((t_new, MLA_HEADS * MLA_V)),
        scratch_shapes=[pltpu.VMEM((rows, 1), F32), pltpu.VMEM((rows, 1), F32), pltpu.VMEM((rows, MLA_KV_RANK), F32),
                        pltpu.VMEM((2, keys, MLA_KV_RANK), F32), pltpu.VMEM((2, MLA_ROPE, keys), F32),
                        pltpu.VMEM((2, MLA_HEADS, keys), F32), pltpu.SemaphoreType.DMA((2, 3))],
    )
    return pl.pallas_call(
        functools.partial(_decode_body, pages=pages, n_chunks=n_chunks, t_new=t_new),
        grid_spec=grid_spec,
        out_shape=jax.ShapeDtypeStruct((bs, t_new, MLA_HEADS * MLA_V), F32),
        compiler_params=_cparams("arbitrary"),
        name="mla_sample_attn",
    )(page_table.reshape(-1), ql, qr, c_new, kr_new, ir_new, wuv_h, lat_pool, krt_pool, irt_pool)


def _gla_body(gq_ref, gk_ref, gv_ref, r_ref, alr_ref, wga_ref, bga_ref, gn_ref, *rest, chunk, sub, nb, carry):
    if carry:
        y_ref, s_out_ref = rest
        s_in_ref = s_out_ref

        @pl.when(pl.program_id(0) == 0)
        def _():
            s_out_ref[...] = jnp.zeros(s_out_ref.shape, F32)
    else:
        s_in_ref, y_ref, s_out_ref = rest
    row = lax.broadcasted_iota(jnp.int32, (chunk, chunk), 0)
    col = lax.broadcasted_iota(jnp.int32, (chunk, chunk), 1)
    tri = (col <= row).astype(F32)
    ones_cv = jnp.ones((chunk, GLA_DV), F32)
    brow = lax.broadcasted_iota(jnp.int32, (sub, chunk), 0)
    bcol = lax.broadcasted_iota(jnp.int32, (sub, chunk), 1)
    hi = lax.Precision.HIGHEST
    gn = gn_ref[...]

    def group(bis):
        chains = [(n, h) for n in range(len(bis)) for h in range(GLA_HEADS)]
        ks = lambda h: slice(h * GLA_DK, (h + 1) * GLA_DK)
        vs = lambda h: slice(h * GLA_DV, (h + 1) * GLA_DV)
        gate = []
        for bi in bis:
            a = _dot(alr_ref[bi].astype(BF16), wga_ref[...]) + bga_ref[...]
            gate.append((jnp.minimum(a, 0.0) - jnp.log1p(jnp.exp(-jnp.abs(a)))) / GLA_TAU)
        cum = [_dot(tri, g, precision=hi) for g in gate]
        q = {(n, h): gq_ref[bis[n], :, ks(h)] * GLA_DK ** -0.5 for n, h in chains}
        k = {(n, h): gk_ref[bis[n], :, ks(h)] for n, h in chains}
        vb = {(n, h): gv_ref[bis[n], :, vs(h)].astype(BF16) for n, h in chains}
        s0 = {(n, h): s_in_ref[bis[n], h] for n, h in chains}
        bc = {(n, h): cum[n][:, ks(h)] for n, h in chains}
        o = {ch: _dot((q[ch] * jnp.exp(bc[ch])).astype(BF16), s0[ch].astype(BF16)) for ch in chains}
        for n, h in chains:
            ch = (n, h)
            dec = jnp.exp(_dot_tn(gate[n][:, ks(h)], ones_cv, precision=hi))
            kd = (k[ch] * jnp.exp(bc[ch][chunk - 1:chunk] - bc[ch])).astype(BF16)
            s_out_ref[bis[n], h] = dec * s0[ch] + _dot_tn(kd, vb[ch])
        for blk in range(chunk // sub):
            r0 = blk * sub
            a_blk = {}
            for ch in chains:
                qb = q[ch][r0:r0 + sub]
                bb = bc[ch][r0:r0 + sub]
                ab = jnp.zeros((sub, chunk), F32)
                for s in range(sub):
                    e = jnp.exp(jnp.minimum(bb - bb[s:s + 1], 0.0))
                    cs = jnp.sum(qb * e * k[ch][r0 + s:r0 + s + 1], -1, keepdims=True)
                    ab = jnp.where(bcol == r0 + s, jnp.where(brow >= s, cs, 0.0), ab)
                if blk > 0:
                    b0 = bc[ch][r0 - 1:r0]
                    qs = (qb * jnp.exp(bb - b0)).astype(BF16)
                    ksc = (k[ch] * jnp.exp(jnp.minimum(b0 - bc[ch], 0.0))).astype(BF16)
                    ab = jnp.where(bcol < r0, _dot_nt(qs, ksc), ab)
                a_blk[ch] = ab
            for n, h in chains:
                ch = (n, h)
                ob = o[ch][r0:r0 + sub] + _dot(a_blk[ch].astype(BF16), vb[ch])
                rr = r_ref[bis[n], r0:r0 + sub, vs(h)]
                y_ref[bis[n], r0:r0 + sub, vs(h)] = _rms(ob, gn) * (rr * jax.nn.sigmoid(rr))

    if carry:
        group(list(range(nb)))
    else:
        def rows(i, carry_):
            group([i * GLA_ROWS_PER_ITER + j for j in range(GLA_ROWS_PER_ITER)])
            return carry_

        lax.fori_loop(0, nb // GLA_ROWS_PER_ITER, rows, 0)


def _gla(z3, state, wga, bga, gn, chunk, sub, nb):
    b, s, _ = z3.shape
    carry = state is None
    if carry:
        grid = (s // chunk,)
        zmap = lambda blk: (lambda c: (0, c, blk))
        smap = lambda c: (0, 0, 0, 0)
        nb = b
    else:
        grid = (b // nb,)
        zmap = lambda blk: (lambda i: (i, 0, blk))
        smap = lambda i: (i, 0, 0, 0)
    zspec = lambda wb: pl.BlockSpec((nb, chunk, wb[0]), zmap(wb[1]))
    const = lambda shape: pl.BlockSpec(shape, lambda i: (0,) * len(shape))
    sspec = pl.BlockSpec((nb, GLA_HEADS, GLA_DK, GLA_DV), smap)
    in_specs = [zspec(ZB_GQ), zspec(ZB_GK), zspec(ZB_GV), zspec(ZB_R), zspec(ZB_ALR),
                const(wga.shape), const(bga.shape), const(gn.shape)]
    args = [z3, z3, z3, z3, z3, wga, bga, gn]
    if not carry:
        in_specs.append(sspec)
        args.append(state)
    return pl.pallas_call(
        functools.partial(_gla_body, chunk=chunk, sub=sub, nb=nb, carry=carry),
        grid=grid,
        in_specs=in_specs,
        out_specs=[pl.BlockSpec((nb, chunk, GLA_HEADS * GLA_DV), zmap(0)), sspec],
        out_shape=[jax.ShapeDtypeStruct((b, s, GLA_HEADS * GLA_DV), F32),
                   jax.ShapeDtypeStruct((b, GLA_HEADS, GLA_DK, GLA_DV), F32)],
        compiler_params=_cparams("arbitrary"),
        name="gla_prompt" if carry else "gla_sample",
    )(*args)


def _out_proj_body(mla_ref, gla_ref, x_ref, nm_ref, w_ref, o_ref, *, mla_feature_major):
    mla = mla_ref[...].T if mla_feature_major else mla_ref[...]
    half = mla.shape[1]
    mn = _rms(mla, nm_ref[...]).astype(BF16)
    o_ref[...] = x_ref[...] + _dot(mn, w_ref[:half]) + _dot(gla_ref[...].astype(BF16), w_ref[half:])


def _out_proj(mla_o, gla_y, x, nm, w, tm):
    n, d = x.shape
    half = gla_y.shape[1]
    feature_major = mla_o.ndim == 3
    if feature_major:
        per_b = mla_o.shape[2] // tm
        mla_spec = pl.BlockSpec((None, half, tm), lambda i: (i // per_b, 0, lax.rem(i, per_b)))
    else:
        mla_spec = pl.BlockSpec((tm, half), lambda i: (i, 0))
    return pl.pallas_call(
        functools.partial(_out_proj_body, mla_feature_major=feature_major),
        grid=(n // tm,),
        in_specs=[mla_spec, pl.BlockSpec((tm, half), lambda i: (i, 0)),
                  pl.BlockSpec((tm, d), lambda i: (i, 0)), pl.BlockSpec((1, half), lambda i: (0, 0)),
                  pl.BlockSpec(w.shape, lambda i: (0, 0))],
        out_specs=pl.BlockSpec((tm, d), lambda i: (i, 0)),
        out_shape=jax.ShapeDtypeStruct((n, d), F32),
        compiler_params=_cparams("parallel"),
        name="out_proj",
    )(mla_o, gla_y, x, nm.reshape(1, half), w)


def _top_desc(s, n):
    vals = []
    w = s
    for _ in range(n):
        m = jnp.max(w, axis=0, keepdims=True)
        vals.append(m)
        w = jnp.where(w == m, NEG_INF, w)
    return vals


def _peer_route_body(h_ref, g_ref, wqt_ref, k1_ref, k2_ref, xnt_ref, s2_ref, e2_ref, thr_ref, e1z_ref):
    xnt = _rms(h_ref[...], g_ref[...]).T.astype(BF16)
    xnt_ref[...] = xnt
    qt = _dot(wqt_ref[...], xnt).astype(BF16)
    half = PEER_N_KEYS
    for h in range(PEER_HEADS):
        s1 = _dot(k1_ref[...], qt[(2 * h) * half:(2 * h + 1) * half])
        s2 = _dot(k2_ref[...], qt[(2 * h + 1) * half:(2 * h + 2) * half])
        v1 = _top_desc(s1, PEER_TOPK)
        v2 = _top_desc(s2, PEER_TOPK)
        pairs = [(a, b) for a in range(PEER_TOPK) for b in range(PEER_TOPK // (a + 1))]
        sums = {ab: v1[ab[0]] + v2[ab[1]] for ab in pairs}
        npad = -len(pairs) % 8
        stack = jnp.concatenate([sums[ab] for ab in pairs] + [jnp.full_like(v1[0], NEG_INF)] * npad, 0)
        tau = _top_desc(stack, PEER_TOPK)[-1]
        top = v1[0] + v2[0]
        zsum = jnp.sum(jnp.where(stack >= tau, jnp.exp(stack - top), 0.0), axis=0, keepdims=True)
        thr = jnp.full(s1.shape, POS_INF, F32)
        for a in range(PEER_TOPK):
            ta = jnp.full_like(tau, POS_INF)
            for b in range(PEER_TOPK // (a + 1)):
                ta = jnp.minimum(ta, jnp.where(sums[(a, b)] >= tau, v2[b], POS_INF))
            thr = jnp.where(s1 == v1[a], ta, thr)
        e2 = jnp.exp(s2 - v2[0])
        e1z = jnp.exp(s1 - v1[0]) / zsum
        for lg in range(thr.shape[1] // 128):
            lanes = slice(lg * 128, (lg + 1) * 128)
            s2_ref[h, lg] = s2[:, lanes]
            e2_ref[h, lg] = e2[:, lanes]
            thr_ref[h, lg] = thr[:, lanes]
            e1z_ref[h, lg] = e1z[:, lanes]


def _peer_route(h, g, wqt, k1, k2, tt):
    n, d = h.shape
    per_group = jax.ShapeDtypeStruct((PEER_HEADS, n // 128, PEER_N_KEYS, 128), F32)
    gspec = pl.BlockSpec((PEER_HEADS, tt // 128, PEER_N_KEYS, 128), lambda i: (0, i, 0, 0))
    return pl.pallas_call(
        _peer_route_body,
        grid=(n // tt,),
        in_specs=[pl.BlockSpec((tt, d), lambda i: (i, 0)), pl.BlockSpec((1, d), lambda i: (0, 0)),
                  pl.BlockSpec(wqt.shape, lambda i: (0, 0)), pl.BlockSpec(k1.shape, lambda i: (0, 0)),
                  pl.BlockSpec(k2.shape, lambda i: (0, 0))],
        out_specs=[pl.BlockSpec((d, tt), lambda i: (0, i)), gspec, gspec, gspec, gspec],
        out_shape=[jax.ShapeDtypeStruct((d, n), BF16), per_group, per_group, per_group, per_group],
        compiler_params=_cparams("parallel"),
        name="peer_route",
    )(h, g.reshape(1, d), wqt, k1, k2)


def _peer_dense_body(xnt_ref, u_ref, vt_ref, s2_ref, e2_ref, thr_ref, e1z_ref, yt_ref, act_sc, ca_sc):
    @pl.when(pl.program_id(1) == 0)
    def _():
        yt_ref[...] = jnp.zeros(yt_ref.shape, F32)

    n_lg, te, _ = act_sc.shape
    act = _dot(u_ref[...], xnt_ref[...])
    act = 0.5 * act * (1.0 + lax.erf(act * (2.0 ** -0.5)))
    for lg in range(n_lg):
        act_sc[lg] = act[:, lg * 128:(lg + 1) * 128]
    groups = te // PEER_N_KEYS
    n_sub = PEER_N_KEYS // PEER_SUB

    def gate_block(blk, carry):
        lg = blk // n_sub
        k0 = pl.multiple_of(lax.rem(blk, n_sub) * PEER_SUB, PEER_SUB)
        coef = [jnp.zeros((PEER_SUB, 128), F32)] * groups
        for h in range(PEER_HEADS):
            s2b = s2_ref[h, lg, pl.ds(k0, PEER_SUB), :]
            e2b = e2_ref[h, lg, pl.ds(k0, PEER_SUB), :]
            for gi in range(groups):
                thr = jnp.broadcast_to(thr_ref[h, lg, gi:gi + 1, :], (PEER_SUB, 128))
                e1z = jnp.broadcast_to(e1z_ref[h, lg, gi:gi + 1, :], (PEER_SUB, 128))
                coef[gi] = coef[gi] + jnp.where(s2b >= thr, e2b, 0.0) * e1z
        for gi in range(groups):
            rows = pl.ds(pl.multiple_of(gi * PEER_N_KEYS + k0, PEER_SUB), PEER_SUB)
            ca_sc[lg, rows, :] = (coef[gi] * act_sc[lg, rows, :]).astype(BF16)
        return carry

    lax.fori_loop(0, n_lg * n_sub, gate_block, 0)
    ca = jnp.concatenate([ca_sc[lg] for lg in range(n_lg)], 1)
    yt_ref[...] += _dot(vt_ref[...], ca)


def _peer_dense(xnt, u, vt, s2, e2, thr, e1z, tt, te):
    d, n = xnt.shape
    ne = u.shape[0] // te
    gpt = te // PEER_N_KEYS
    full = pl.BlockSpec((PEER_HEADS, tt // 128, PEER_N_KEYS, 128), lambda i, j: (0, i, 0, 0))
    part = pl.BlockSpec((PEER_HEADS, tt // 128, gpt, 128), lambda i, j: (0, i, j, 0))
    return pl.pallas_call(
        _peer_dense_body,
        grid=(n // tt, ne),
        in_specs=[pl.BlockSpec((d, tt), lambda i, j: (0, i)), pl.BlockSpec((te, d), lambda i, j: (j, 0)),
                  pl.BlockSpec((d, te), lambda i, j: (0, j)), full, full, part, part],
        out_specs=pl.BlockSpec((d, tt), lambda i, j: (0, i)),
        out_shape=jax.ShapeDtypeStruct((d, n), F32),
        scratch_shapes=[pltpu.VMEM((tt // 128, te, 128), F32), pltpu.VMEM((tt // 128, te, 128), BF16)],
        compiler_params=_cparams("parallel", "arbitrary"),
        name="peer_dense",
    )(xnt, u, vt, s2, e2, thr, e1z)


def _ple_body(h_ref, yt_ref, p_ref, g_ref, wg_ref, wp_ref, o_ref):
    h2 = h_ref[...] + yt_ref[...].T
    gate = jax.nn.sigmoid(_dot(_rms(h2, g_ref[...]).astype(BF16), wg_ref[...]))
    o_ref[...] = h2 + gate * _dot(p_ref[...].astype(BF16), wp_ref[...])


def _ple(h, yt, p, g, wg, wp, tm):
    n, d = h.shape
    return pl.pallas_call(
        _ple_body,
        grid=(n // tm,),
        in_specs=[pl.BlockSpec((tm, d), lambda i: (i, 0)), pl.BlockSpec((d, tm), lambda i: (0, i)),
                  pl.BlockSpec((tm, p.shape[1]), lambda i: (i, 0)), pl.BlockSpec((1, d), lambda i: (0, 0)),
                  pl.BlockSpec(wg.shape, lambda i: (0, 0)), pl.BlockSpec(wp.shape, lambda i: (0, 0))],
        out_specs=pl.BlockSpec((tm, d), lambda i: (i, 0)),
        out_shape=jax.ShapeDtypeStruct((n, d), F32),
        compiler_params=_cparams("parallel"),
        name="ple",
    )(h, yt, p, g.reshape(1, d), wg, wp)


def _rope_table(pos):
    half = MLA_ROPE // 2
    inv_freq = ROPE_THETA ** (-jnp.arange(half, dtype=F32) / half)
    ang = pos.astype(F32)[:, None] * inv_freq
    cos, sin = jnp.cos(ang), jnp.sin(ang)
    return jnp.concatenate([cos, cos, sin, sin], -1)


def _tile(n, t):
    return t if n % t == 0 else n


def _finish(x2, mla_o, gla_y, p2, w):
    n = x2.shape[0]
    h = _out_proj(mla_o, gla_y, x2, w["norm_mla_out"], w["w_out"], _tile(n, TM_OUT))
    xnt, s2, e2, thr, e1z = _peer_route(h, w["norm_ffn"], w["wqt"], w["k1"], w["k2"], _tile(n, TT_ROUTE))
    yt = _peer_dense(xnt, w["u"], w["vt"], s2, e2, thr, e1z, _tile(n, TT_DENSE), TE_DENSE)
    return _ple(h, yt, p2, w["norm_ple"], w["w_ple_gate"], w["w_ple_proj"], _tile(n, TM_PLE))


def kernel(x_prompt, x_sample, cache_kv_latent, cache_k_rope, cache_k_inv_rms, state_gla, page_table, p_prompt,
           p_sample, norm_mix, w_in, norm_q_lat, w_uq, norm_kv_lat, w_uk, w_uv, qk_gain_q, qk_gain_k, norm_mla_out,
           w_gla_a, b_gla_a, norm_gla_out, w_out, norm_ffn, peer_w_q, peer_keys1, peer_keys2, peer_u, peer_v,
           norm_ple, w_ple_gate, w_ple_proj):
    depth = w_in.shape[0]
    b, s, d = x_prompt.shape
    bs, t_new, _ = x_sample.shape
    n_pages = page_table.shape[1]
    page = cache_kv_latent.shape[2]
    past_len = n_pages * page
    tab_p = _rope_table(jnp.arange(s))
    tab_s = _rope_table(past_len + jnp.arange(t_new))
    hp, hs = x_prompt.reshape(b * s, d), x_sample.reshape(bs * t_new, d)
    outs = [[] for _ in range(8)]
    for i in range(depth):
        wi = w_in[i]
        cuts = [0, 512, 768, 832, 1344, 1856, 2880, 3904, 3920]
        cq_w, ckv_w, kr_w, gq_w, gk_w, gv_w, r_w, alr_w = [wi[:, cuts[j]:cuts[j + 1]] for j in range(8)]
        w_in_ext = jnp.concatenate(
            [cq_w, gq_w, gk_w, ckv_w, kr_w, _rot_cols(kr_w), alr_w,
             jnp.zeros((d, 128 - GLA_GATE_RANK), F32), gv_w, r_w], 1).astype(BF16)
        wuq3 = w_uq[i].reshape(MLA_Q_RANK, MLA_HEADS, MLA_QK)
        wuq_ext = jnp.concatenate([wuq3, _rot_cols(wuq3[..., MLA_NOPE:])], -1).reshape(MLA_Q_RANK, -1).astype(BF16)
        gq, gk = qk_gain_q[i], qk_gain_k[i]
        gq_ext = jnp.concatenate([gq, _rot_gain(gq[MLA_NOPE:])]).reshape(1, -1)
        gk_ext = jnp.concatenate([gk, _rot_gain(gk[MLA_NOPE:])]).reshape(1, -1)
        wuk = w_uk[i].reshape(MLA_KV_RANK, -1).astype(BF16)
        wuv = w_uv[i].reshape(MLA_KV_RANK, -1).astype(BF16)
        wuk_t = jnp.transpose(w_uk[i], (1, 2, 0)).astype(BF16)
        wuv_h = jnp.transpose(w_uv[i], (1, 0, 2)).astype(BF16)
        wga = jnp.concatenate([w_gla_a[i], jnp.zeros((128 - GLA_GATE_RANK, w_gla_a.shape[2]), F32)], 0).astype(BF16)
        bga = b_gla_a[i].reshape(1, -1)
        gn = norm_gla_out[i].reshape(1, -1)
        w = {"norm_mla_out": norm_mla_out[i], "w_out": w_out[i].astype(BF16), "norm_ffn": norm_ffn[i],
             "wqt": peer_w_q[i].T.astype(BF16), "k1": peer_keys1[i].astype(BF16), "k2": peer_keys2[i].astype(BF16),
             "u": peer_u[i].astype(BF16), "vt": peer_v[i].T.astype(BF16), "norm_ple": norm_ple[i],
             "w_ple_gate": w_ple_gate[i].astype(BF16), "w_ple_proj": w_ple_proj[i].astype(BF16)}
        prep = (norm_q_lat[i], norm_kv_lat[i], gq_ext, gk_ext, wuq_ext, wuk)

        n = b * s
        z = _norm_matmul(hp, norm_mix[i], w_in_ext, _tile(n, TM_IN), TN_IN)
        tm = _tile(s, TM_PREP)
        qt, kt, vt, lat, kr, ir = _mla_prep(z, tab_p, s // tm, *prep, wuv, tm, sample=False)
        tq = _tile(s, FLASH_TILE)
        mla_o = _flash(qt.reshape(b, s, -1), kt.reshape(b, s, -1), vt, tq, FLASH_HEADS)
        gla_y, s_fin = _gla(z.reshape(b, s, Z_COLS), None, wga, bga, gn, GLA_CHUNK, GLA_SUB, b)
        hp_next = _finish(hp, mla_o, gla_y.reshape(n, -1), p_prompt[i].reshape(n, -1), w)
        outs[0].append(lat.reshape(b, s, -1))
        outs[1].append(kr.reshape(b, s, -1))
        outs[2].append(ir[:, :MLA_HEADS].reshape(b, s, -1))
        outs[3].append(s_fin)

        n = bs * t_new
        z = _norm_matmul(hs, norm_mix[i], w_in_ext, _tile(n, TM_IN), TN_IN)
        tm = _tile(n, TM_PREP)
        tab = jnp.tile(tab_s, (tm // t_new, 1))
        ql, qr, lat, kr, ir = _mla_prep(z, tab, 1, *prep, wuk_t, tm, sample=True)
        mla_o = _decode(page_table, ql.reshape(MLA_HEADS, bs, t_new, -1), qr.reshape(MLA_HEADS, bs, t_new, -1),
                        lat.reshape(bs, t_new, -1), kr.reshape(bs, t_new, -1), ir.reshape(bs, t_new, -1), wuv_h,
                        cache_kv_latent[i], jnp.transpose(cache_k_rope[i], (0, 2, 1)),
                        jnp.transpose(cache_k_inv_rms[i], (0, 2, 1)), _tile(n_pages, DECODE_PAGES))
        gla_y, s_new = _gla(z.reshape(bs, t_new, Z_COLS), state_gla[i], wga, bga, gn, t_new, t_new,
                            _tile(bs, GLA_SAMPLE_ROWS))
        hs_next = _finish(hs, mla_o.reshape(n, -1), gla_y.reshape(n, -1), p_sample[i].reshape(n, -1), w)
        outs[4].append(lat.reshape(bs, t_new, -1))
        outs[5].append(kr.reshape(bs, t_new, -1))
        outs[6].append(ir[:, :MLA_HEADS].reshape(bs, t_new, -1))
        outs[7].append(s_new)
        hp, hs = hp_next, hs_next
    return (hp.reshape(b, s, d), hs.reshape(bs, t_new, d)) + tuple(jnp.stack(o, 0) for o in outs)
```

```python
import functools

import jax
import jax.numpy as jnp
from jax import lax
from jax.experimental import pallas as pl
from jax.experimental.pallas import tpu as pltpu

F32 = jnp.float32
BF16 = jnp.bfloat16
EPS = 1e-6
NEG_INF = float("-inf")
POS_INF = float("inf")

MLA_HEADS = 8
MLA_NOPE = 128
MLA_ROPE = 64
MLA_QK = MLA_NOPE + MLA_ROPE
MLA_V = 128
MLA_Q_RANK = 512
MLA_KV_RANK = 256
MLA_SCALE = MLA_QK ** -0.5
LOG2E = 1.4426950408889634
MLA_HEAD_PAD = 256
ROPE_THETA = 10000.0
GLA_HEADS = 4
GLA_DK = 128
GLA_DV = 256
GLA_GATE_RANK = 16
GLA_TAU = 16.0
GLA_CHUNK = 64
GLA_SUB = 16
PEER_HEADS = 8
PEER_N_KEYS = 128
PEER_TOPK = 16
PEER_SUB = 32
PLE_DIM = 256

Z_COLS = 4096
ZB_CQ = (512, 0)
ZB_GQ = (512, 1)
ZB_GK = (512, 2)
ZB_CKV = (256, 6)
ZB_KR = (128, 14)
ZB_ALR = (128, 15)
ZB_GV = (1024, 2)
ZB_R = (1024, 3)

V7X_VMEM_LIMIT = 56 * 1024 * 1024

TM_IN, TN_IN = 1024, 1024
TM_PREP = 256
FLASH_TILE = 512
FLASH_HEADS = 8
FLASH_SUB = 256
DECODE_PAGES = 32
DECODE_PARTS = 4
GLA_SAMPLE_ROWS = 8
GLA_ROWS_PER_ITER = 2
TM_OUT = 256
TT_ROUTE = 256
TT_DENSE, TE_DENSE = 512, 1024
TM_PLE = 256


def _cparams(*sem):
    return pltpu.CompilerParams(dimension_semantics=sem, vmem_limit_bytes=V7X_VMEM_LIMIT)


def _dot(a, b, **kw):
    return jnp.dot(a, b, preferred_element_type=F32, **kw)


def _dot_nt(a, b):
    return lax.dot_general(a, b, (((1,), (1,)), ((), ())), preferred_element_type=F32)


def _dot_tn(a, b, **kw):
    return lax.dot_general(a, b, (((0,), (0,)), ((), ())), preferred_element_type=F32, **kw)


def _rms(x, g):
    return x * lax.rsqrt(jnp.mean(x * x, -1, keepdims=True) + EPS) * g


def _rot_cols(w):
    h = w.shape[-1] // 2
    return jnp.concatenate([-w[..., h:], w[..., :h]], -1)


def _rot_gain(g):
    h = g.shape[-1] // 2
    return jnp.concatenate([g[..., h:], g[..., :h]], -1)


def _norm_matmul_body(x_ref, g_ref, w_ref, o_ref, xn_sc):
    @pl.when(pl.program_id(1) == 0)
    def _():
        xn_sc[...] = _rms(x_ref[...], g_ref[...]).astype(BF16)

    o_ref[...] = _dot(xn_sc[...], w_ref[...])


def _norm_matmul(x, g, w, tm, tn):
    n, d = x.shape
    nc = w.shape[1]
    return pl.pallas_call(
        _norm_matmul_body,
        grid=(n // tm, nc // tn),
        in_specs=[pl.BlockSpec((tm, d), lambda i, j: (i, 0)),
                  pl.BlockSpec((1, d), lambda i, j: (0, 0)),
                  pl.BlockSpec((d, tn), lambda i, j: (0, j))],
        out_specs=pl.BlockSpec((tm, tn), lambda i, j: (i, j)),
        out_shape=jax.ShapeDtypeStruct((n, nc), F32),
        scratch_shapes=[pltpu.VMEM((tm, d), BF16)],
        compiler_params=_cparams("parallel", "arbitrary"),
        name="in_proj",
    )(x, g.reshape(1, d), w)


def _mla_prep_body(cq_ref, ckv_ref, krb_ref, tab_ref, nq_ref, nkv_ref, gq_ref, gk_ref, wuq_ref, wuk_ref, wx_ref,
                   *outs, sample):
    if sample:
        ql_ref, qr_ref, lat_ref, kr_ref, ir_ref = outs
    else:
        qt_ref, kt_ref, vt_ref, lat_ref, kr_ref, ir_ref = outs
    tab = tab_ref[...]
    gq = gq_ref[...]
    gk = gk_ref[...]
    cqn = _rms(cq_ref[...], nq_ref[...]).astype(BF16)
    qraw = _dot(cqn, wuq_ref[...])
    c = _rms(ckv_ref[...], nkv_ref[...])
    lat_ref[...] = c
    cb = c.astype(BF16)
    kn = _dot(cb, wuk_ref[...])
    if not sample:
        vt_ref[...] = _dot(cb, wx_ref[...]).T.astype(BF16)
    krb = krb_ref[...]
    lane = lax.broadcasted_iota(jnp.int32, krb.shape, 1)
    ss_kr = jnp.sum(jnp.where(lane < MLA_ROPE, krb * krb, 0.0), -1, keepdims=True)
    t = krb * gk[:, MLA_NOPE:] * tab
    kr2 = t + pltpu.roll(t, MLA_ROPE, 1)
    kr_ref[...] = kr2[:, :MLA_ROPE]
    krz = jnp.where(lane < MLA_ROPE, kr2, 0.0)
    lane_q = lax.broadcasted_iota(jnp.int32, (krb.shape[0], MLA_HEAD_PAD), 1)
    ir_acc = jnp.zeros(krb.shape, F32)
    for h in range(MLA_HEADS):
        knh = kn[:, h * MLA_NOPE:(h + 1) * MLA_NOPE]
        ss = jnp.sum(knh * knh, -1, keepdims=True) + ss_kr
        inv = lax.rsqrt(ss / MLA_QK + EPS)
        ir_acc = jnp.where(lane == h, inv, ir_acc)
        qh = qraw[:, h * MLA_HEAD_PAD:(h + 1) * MLA_HEAD_PAD]
        ssq = jnp.sum(jnp.where(lane_q < MLA_QK, qh * qh, 0.0), -1, keepdims=True)
        qinv = lax.rsqrt(ssq / MLA_QK + EPS)
        qn = qh[:, :MLA_NOPE] * qinv * gq[:, :MLA_NOPE]
        tq = qh[:, MLA_NOPE:] * qinv * gq[:, MLA_NOPE:] * tab
        qr2 = tq + pltpu.roll(tq, MLA_ROPE, 1)
        if sample:
            qg = (qn * gk[:, :MLA_NOPE]).astype(BF16)
            ql_ref[h] = _dot(qg, wx_ref[h]) * MLA_SCALE
            qr_ref[h] = qr2[:, :MLA_ROPE] * MLA_SCALE
        else:
            lo = h * MLA_HEAD_PAD
            qt_ref[:, lo:lo + MLA_NOPE] = (qn * (MLA_SCALE * LOG2E)).astype(BF16)
            qt_ref[:, lo + MLA_NOPE:lo + MLA_HEAD_PAD] = (qr2 * (MLA_SCALE * LOG2E)).astype(BF16)
            kt_ref[:, lo:lo + MLA_NOPE] = (knh * gk[:, :MLA_NOPE] * inv).astype(BF16)
            kt_ref[:, lo + MLA_NOPE:lo + MLA_HEAD_PAD] = (krz * inv).astype(BF16)
    ir_ref[...] = ir_acc


def _mla_prep(z, tab, tab_blocks, nq, nkv, gq_ext, gk_ext, wuq_ext, wuk, wx, tm, sample):
    n = z.shape[0]
    hp = MLA_HEADS * MLA_HEAD_PAD

    def const(shape):
        return pl.BlockSpec(shape, lambda i: (0,) * len(shape))

    in_specs = [pl.BlockSpec((tm, ZB_CQ[0]), lambda i: (i, ZB_CQ[1])),
                pl.BlockSpec((tm, ZB_CKV[0]), lambda i: (i, ZB_CKV[1])),
                pl.BlockSpec((tm, ZB_KR[0]), lambda i: (i, ZB_KR[1])),
                pl.BlockSpec((tm, 128), lambda i: (i % tab_blocks, 0)),
                const((1, MLA_Q_RANK)), const((1, MLA_KV_RANK)), const((1, MLA_HEAD_PAD)), const((1, MLA_HEAD_PAD)),
                const(wuq_ext.shape), const(wuk.shape), const(wx.shape)]
    row = lambda w: pl.BlockSpec((tm, w), lambda i: (i, 0))
    tail_specs = [row(MLA_KV_RANK), row(MLA_ROPE), row(128)]
    tail_shapes = [jax.ShapeDtypeStruct((n, MLA_KV_RANK), F32), jax.ShapeDtypeStruct((n, MLA_ROPE), F32),
                   jax.ShapeDtypeStruct((n, 128), F32)]
    if sample:
        head = lambda w: pl.BlockSpec((MLA_HEADS, tm, w), lambda i: (0, i, 0))
        out_specs = [head(MLA_KV_RANK), head(MLA_ROPE)] + tail_specs
        out_shape = [jax.ShapeDtypeStruct((MLA_HEADS, n, MLA_KV_RANK), F32),
                     jax.ShapeDtypeStruct((MLA_HEADS, n, MLA_ROPE), F32)] + tail_shapes
    else:
        out_specs = [row(hp), row(hp), pl.BlockSpec((MLA_HEADS * MLA_V, tm), lambda i: (0, i))] + tail_specs
        out_shape = [jax.ShapeDtypeStruct((n, hp), BF16), jax.ShapeDtypeStruct((n, hp), BF16),
                     jax.ShapeDtypeStruct((MLA_HEADS * MLA_V, n), BF16)] + tail_shapes
    return pl.pallas_call(
        functools.partial(_mla_prep_body, sample=sample),
        grid=(n // tm,),
        in_specs=in_specs, out_specs=out_specs, out_shape=out_shape,
        compiler_params=_cparams("parallel"),
        name="mla_prep_sample" if sample else "mla_prep_prompt",
    )(z, z, z, tab, nq.reshape(1, -1), nkv.reshape(1, -1), gq_ext, gk_ext, wuq_ext, wuk, wx)


def _flash_body(q_ref, k_ref, vt_ref, o_ref, m_sc, l_sc, acc_sc, *, heads, sub):
    i = pl.program_id(2)
    j = pl.program_id(3)

    @pl.when(j == 0)
    def _():
        m_sc[...] = jnp.full(m_sc.shape, NEG_INF, F32)
        l_sc[...] = jnp.zeros(l_sc.shape, F32)
        acc_sc[...] = jnp.zeros(acc_sc.shape, F32)

    tile = q_ref.shape[0]

    def step(diagonal):
        def q_cols(c, carry):
            r0 = pl.multiple_of(c * sub, sub)
            cols = pl.ds(r0, sub)
            vrows = [slice(h * MLA_V, (h + 1) * MLA_V) for h in range(heads)]
            prev = [(m_sc[h, :, cols], l_sc[h, :, cols], acc_sc[vrows[h], cols]) for h in range(heads)]
            new = []

            def scores(h):
                qk = slice(h * MLA_HEAD_PAD, (h + 1) * MLA_HEAD_PAD)
                return _dot_nt(k_ref[:, qk], q_ref[cols, qk])

            ahead = 2
            pending = [scores(h) for h in range(min(ahead, heads))]
            for h in range(heads):
                s = pending.pop(0)
                if diagonal:
                    keep = (lax.broadcasted_iota(jnp.int32, s.shape, 0)
                            <= r0 + lax.broadcasted_iota(jnp.int32, s.shape, 1))
                    s = jnp.where(keep, s, NEG_INF)
                m_prev, l_prev, acc_prev = prev[h]
                m_new = jnp.maximum(m_prev, jnp.max(s, 0, keepdims=True))
                alpha = jnp.exp2(m_prev - m_new)
                p = jnp.exp2(s - m_new)
                l_new = alpha * l_prev + jnp.sum(p, 0, keepdims=True)
                acc = alpha * acc_prev + _dot(vt_ref[vrows[h], :], p.astype(BF16))
                new.append((m_new, l_new, acc))
                if h + ahead < heads:
                    pending.append(scores(h + ahead))
            for h, (m_new, l_new, acc) in enumerate(new):
                if diagonal:
                    o_ref[vrows[h], cols] = acc / l_new
                else:
                    m_sc[h, :, cols] = m_new
                    l_sc[h, :, cols] = l_new
                    acc_sc[vrows[h], cols] = acc
            return carry

        lax.fori_loop(0, tile // sub, q_cols, 0)

    pl.when(j < i)(lambda: step(False))
    pl.when(j == i)(lambda: step(True))


def _flash(qt, kt, vt, tile, heads):
    b, s, _ = qt.shape
    nt = s // tile
    return pl.pallas_call(
        functools.partial(_flash_body, heads=heads, sub=min(FLASH_SUB, tile)),
        grid=(b, MLA_HEADS // heads, nt, nt),
        in_specs=[pl.BlockSpec((None, tile, heads * MLA_HEAD_PAD), lambda bb, h, i, j: (bb, i, h)),
                  pl.BlockSpec((None, tile, heads * MLA_HEAD_PAD), lambda bb, h, i, j: (bb, jnp.minimum(j, i), h)),
                  pl.BlockSpec((heads * MLA_V, tile), lambda bb, h, i, j: (h, bb * nt + jnp.minimum(j, i)))],
        out_specs=pl.BlockSpec((None, heads * MLA_V, tile), lambda bb, h, i, j: (bb, h, i)),
        out_shape=jax.ShapeDtypeStruct((b, MLA_HEADS * MLA_V, s), F32),
        scratch_shapes=[pltpu.VMEM((heads, 1, tile), F32), pltpu.VMEM((heads, 1, tile), F32),
                        pltpu.VMEM((heads * MLA_V, tile), F32)],
        compiler_params=_cparams("parallel", "parallel", "parallel", "arbitrary"),
        name="mla_prompt_attn",
    )(qt, kt, vt)


def _decode_body(pt_ref, ql_ref, qr_ref, cn_ref, krn_ref, irn_ref, wuv_ref, lat_pool, krt_pool, irt_pool,
                 o_ref, m_sc, l_sc, acc_sc, lat_buf, krt_buf, irt_buf, sems, *, pages, n_chunks, t_new):
    rows = MLA_HEADS * t_new
    page = lat_buf.shape[1] // pages
    t = pl.program_id(0)
    last = pl.num_programs(0) - 1
    c = lax.rem(t, n_chunks)
    slot = lax.rem(t, 2)

    def chunk_copies(step, buf_slot):
        copies = []
        for p in range(pages):
            pg = pt_ref[step * pages + p]
            keys = pl.ds(p * page, page)
            copies.append(pltpu.make_async_copy(lat_pool.at[pg], lat_buf.at[buf_slot, keys, :], sems.at[buf_slot, 0]))
            copies.append(pltpu.make_async_copy(krt_pool.at[pg], krt_buf.at[buf_slot, p], sems.at[buf_slot, 1]))
            copies.append(pltpu.make_async_copy(irt_pool.at[pg], irt_buf.at[buf_slot, p], sems.at[buf_slot, 2]))
        return copies

    @pl.when(t == 0)
    def _():
        for cp in chunk_copies(0, 0):
            cp.start()

    @pl.when(c == 0)
    def _():
        m_sc[...] = jnp.full(m_sc.shape, NEG_INF, F32)
        l_sc[...] = jnp.zeros(l_sc.shape, F32)
        acc_sc[...] = jnp.zeros(acc_sc.shape, F32)

    for cp in chunk_copies(t, slot):
        cp.wait()
    nxt = jnp.minimum(t + 1, last)
    for cp in chunk_copies(nxt, 1 - slot):
        cp.start()

    ql = ql_ref[...].reshape(rows, MLA_KV_RANK).astype(BF16)
    qr = qr_ref[...].reshape(rows, MLA_ROPE).astype(BF16)

    def attend(parts, mask=None):
        ss = []
        for lat_b, krt_b, irt in parts:
            s = _dot_nt(ql, lat_b) + _dot(qr, krt_b)
            s = (s.reshape(MLA_HEADS, t_new, -1) * irt[:, None, :]).reshape(rows, -1)
            ss.append(s if mask is None else jnp.where(mask, s, NEG_INF))
        m_prev = m_sc[...]
        m_new = functools.reduce(jnp.maximum, [jnp.max(s, -1, keepdims=True) for s in ss], m_prev)
        alpha = jnp.exp(m_prev - m_new)
        l = alpha * l_sc[...]
        acc = alpha * acc_sc[...]
        for s, (lat_b, _, _) in zip(ss, parts):
            p = jnp.exp(s - m_new)
            l = l + jnp.sum(p, -1, keepdims=True)
            acc = acc + _dot(p.astype(BF16), lat_b)
        l_sc[...] = l
        acc_sc[...] = acc
        m_sc[...] = m_new

    n_parts = min(DECODE_PARTS, pages)
    pp = pages // n_parts
    attend([(lat_buf[slot, i * pp * page:(i + 1) * pp * page, :].astype(BF16),
             jnp.concatenate([krt_buf[slot, p] for p in range(i * pp, (i + 1) * pp)], 1).astype(BF16),
             jnp.concatenate([irt_buf[slot, p] for p in range(i * pp, (i + 1) * pp)], 1)) for i in range(n_parts)])

    @pl.when(c == n_chunks - 1)
    def _():
        pad = 128 - t_new
        lb = jnp.concatenate([cn_ref[...], jnp.zeros((pad, MLA_KV_RANK), F32)], 0).astype(BF16)
        kb = jnp.concatenate([krn_ref[...], jnp.zeros((pad, MLA_ROPE), F32)], 0).T.astype(BF16)
        ib = jnp.concatenate([irn_ref[...], jnp.zeros((pad, 128), F32)], 0).T[:MLA_HEADS]
        tq = lax.rem(lax.broadcasted_iota(jnp.int32, (rows, 128), 0), t_new)
        key = lax.broadcasted_iota(jnp.int32, (rows, 128), 1)
        attend([(lb, kb, ib)], mask=key <= tq)
        o_lat = acc_sc[...] / l_sc[...]
        for h in range(MLA_HEADS):
            o_ref[:, h * MLA_V:(h + 1) * MLA_V] = _dot(o_lat[h * t_new:(h + 1) * t_new].astype(BF16), wuv_ref[h])

    @pl.when(t == last)
    def _():
        for cp in chunk_copies(last, 1 - slot):
            cp.wait()


def _decode(page_table, ql, qr, c_new, kr_new, ir_new, wuv_h, lat_pool, krt_pool, irt_pool, pages):
    bs, n_pages = page_table.shape
    t_new = c_new.shape[1]
    page = lat_pool.shape[1]
    n_chunks = n_pages // pages
    keys = pages * page
    rows = MLA_HEADS * t_new
    per_b = lambda shape: pl.BlockSpec((None,) + shape, lambda t, pt: (t // n_chunks,) + (0,) * len(shape))
    hbm = pl.BlockSpec(memory_space=pl.ANY)
    in_specs = [pl.BlockSpec((MLA_HEADS, None, t_new, MLA_KV_RANK), lambda t, pt: (0, t // n_chunks, 0, 0)),
                pl.BlockSpec((MLA_HEADS, None, t_new, MLA_ROPE), lambda t, pt: (0, t // n_chunks, 0, 0)),
                per_b((t_new, MLA_KV_RANK)), per_b((t_new, MLA_ROPE)), per_b((t_new, 128)),
                pl.BlockSpec(wuv_h.shape, lambda t, pt: (0, 0, 0)), hbm, hbm, hbm]
    grid_spec = pltpu.PrefetchScalarGridSpec(
        num_scalar_prefetch=1,
        grid=(bs * n_chunks,),
        in_specs=in_specs,
        out_specs=per_b((t_new, MLA_HEADS * MLA_V)),
        scratch_shapes=[pltpu.VMEM((rows, 1), F32), pltpu.VMEM((rows, 1), F32), pltpu.VMEM((rows, MLA_KV_RANK), F32),
                        pltpu.VMEM((2, keys, MLA_KV_RANK), F32), pltpu.VMEM((2, pages, MLA_ROPE, page), F32),
                        pltpu.VMEM((2, pages, MLA_HEADS, page), F32), pltpu.SemaphoreType.DMA((2, 3))],
    )
    return pl.pallas_call(
        functools.partial(_decode_body, pages=pages, n_chunks=n_chunks, t_new=t_new),
        grid_spec=grid_spec,
        out_shape=jax.ShapeDtypeStruct((bs, t_new, MLA_HEADS * MLA_V), F32),
        compiler_params=_cparams("arbitrary"),
        name="mla_sample_attn",
    )(page_table.reshape(-1), ql, qr, c_new, kr_new, ir_new, wuv_h, lat_pool, krt_pool, irt_pool)


def _gla_body(gq_ref, gk_ref, gv_ref, r_ref, alr_ref, wga_ref, bga_ref, gn_ref, *rest, chunk, sub, nb, carry):
    if carry:
        y_ref, s_out_ref = rest
        s_in_ref = s_out_ref

        @pl.when(pl.program_id(0) == 0)
        def _():
            s_out_ref[...] = jnp.zeros(s_out_ref.shape, F32)
    else:
        s_in_ref, y_ref, s_out_ref = rest
    row = lax.broadcasted_iota(jnp.int32, (chunk, chunk), 0)
    col = lax.broadcasted_iota(jnp.int32, (chunk, chunk), 1)
    tri = (col <= row).astype(F32)
    ones_cv = jnp.ones((chunk, GLA_DV), F32)
    brow = lax.broadcasted_iota(jnp.int32, (sub, chunk), 0)
    bcol = lax.broadcasted_iota(jnp.int32, (sub, chunk), 1)
    hi = lax.Precision.HIGHEST
    gn = gn_ref[...]

    def group(bis):
        chains = [(n, h) for n in range(len(bis)) for h in range(GLA_HEADS)]
        ks = lambda h: slice(h * GLA_DK, (h + 1) * GLA_DK)
        vs = lambda h: slice(h * GLA_DV, (h + 1) * GLA_DV)
        gate = []
        for bi in bis:
            a = _dot(alr_ref[bi].astype(BF16), wga_ref[...]) + bga_ref[...]
            gate.append((jnp.minimum(a, 0.0) - jnp.log1p(jnp.exp(-jnp.abs(a)))) / GLA_TAU)
        cum = [_dot(tri, g, precision=hi) for g in gate]
        q = {(n, h): gq_ref[bis[n], :, ks(h)] * GLA_DK ** -0.5 for n, h in chains}
        k = {(n, h): gk_ref[bis[n], :, ks(h)] for n, h in chains}
        vb = {(n, h): gv_ref[bis[n], :, vs(h)].astype(BF16) for n, h in chains}
        s0 = {(n, h): s_in_ref[bis[n], h] for n, h in chains}
        bc = {(n, h): cum[n][:, ks(h)] for n, h in chains}
        o = {ch: _dot((q[ch] * jnp.exp(bc[ch])).astype(BF16), s0[ch].astype(BF16)) for ch in chains}
        for n, h in chains:
            ch = (n, h)
            dec = jnp.exp(_dot_tn(gate[n][:, ks(h)], ones_cv, precision=hi))
            kd = (k[ch] * jnp.exp(bc[ch][chunk - 1:chunk] - bc[ch])).astype(BF16)
            s_out_ref[bis[n], h] = dec * s0[ch] + _dot_tn(kd, vb[ch])
        for blk in range(chunk // sub):
            r0 = blk * sub
            a_blk = {}
            for ch in chains:
                qb = q[ch][r0:r0 + sub]
                bb = bc[ch][r0:r0 + sub]
                ab = jnp.zeros((sub, chunk), F32)
                for s in range(sub):
                    e = jnp.exp(jnp.minimum(bb - bb[s:s + 1], 0.0))
                    cs = jnp.sum(qb * e * k[ch][r0 + s:r0 + s + 1], -1, keepdims=True)
                    ab = jnp.where(bcol == r0 + s, jnp.where(brow >= s, cs, 0.0), ab)
                if blk > 0:
                    b0 = bc[ch][r0 - 1:r0]
                    qs = (qb * jnp.exp(bb - b0)).astype(BF16)
                    ksc = (k[ch] * jnp.exp(jnp.minimum(b0 - bc[ch], 0.0))).astype(BF16)
                    ab = jnp.where(bcol < r0, _dot_nt(qs, ksc), ab)
                a_blk[ch] = ab
            for n, h in chains:
                ch = (n, h)
                ob = o[ch][r0:r0 + sub] + _dot(a_blk[ch].astype(BF16), vb[ch])
                rr = r_ref[bis[n], r0:r0 + sub, vs(h)]
                y_ref[bis[n], r0:r0 + sub, vs(h)] = _rms(ob, gn) * (rr * jax.nn.sigmoid(rr))

    if carry:
        group(list(range(nb)))
    else:
        def rows(i, carry_):
            group([i * GLA_ROWS_PER_ITER + j for j in range(GLA_ROWS_PER_ITER)])
            return carry_

        lax.fori_loop(0, nb // GLA_ROWS_PER_ITER, rows, 0)


def _gla(z3, state, wga, bga, gn, chunk, sub, nb):
    b, s, _ = z3.shape
    carry = state is None
    if carry:
        grid = (s // chunk,)
        zmap = lambda blk: (lambda c: (0, c, blk))
        smap = lambda c: (0, 0, 0, 0)
        nb = b
    else:
        grid = (b // nb,)
        zmap = lambda blk: (lambda i: (i, 0, blk))
        smap = lambda i: (i, 0, 0, 0)
    zspec = lambda wb: pl.BlockSpec((nb, chunk, wb[0]), zmap(wb[1]))
    const = lambda shape: pl.BlockSpec(shape, lambda i: (0,) * len(shape))
    sspec = pl.BlockSpec((nb, GLA_HEADS, GLA_DK, GLA_DV), smap)
    in_specs = [zspec(ZB_GQ), zspec(ZB_GK), zspec(ZB_GV), zspec(ZB_R), zspec(ZB_ALR),
                const(wga.shape), const(bga.shape), const(gn.shape)]
    args = [z3, z3, z3, z3, z3, wga, bga, gn]
    if not carry:
        in_specs.append(sspec)
        args.append(state)
    return pl.pallas_call(
        functools.partial(_gla_body, chunk=chunk, sub=sub, nb=nb, carry=carry),
        grid=grid,
        in_specs=in_specs,
        out_specs=[pl.BlockSpec((nb, chunk, GLA_HEADS * GLA_DV), zmap(0)), sspec],
        out_shape=[jax.ShapeDtypeStruct((b, s, GLA_HEADS * GLA_DV), F32),
                   jax.ShapeDtypeStruct((b, GLA_HEADS, GLA_DK, GLA_DV), F32)],
        compiler_params=_cparams("arbitrary"),
        name="gla_prompt" if carry else "gla_sample",
    )(*args)


def _out_proj_body(mla_ref, gla_ref, x_ref, nm_ref, w_ref, o_ref, *, mla_feature_major):
    mla = mla_ref[...].T if mla_feature_major else mla_ref[...]
    half = mla.shape[1]
    mn = _rms(mla, nm_ref[...]).astype(BF16)
    o_ref[...] = x_ref[...] + _dot(mn, w_ref[:half]) + _dot(gla_ref[...].astype(BF16), w_ref[half:])


def _out_proj(mla_o, gla_y, x, nm, w, tm):
    n, d = x.shape
    half = gla_y.shape[1]
    feature_major = mla_o.ndim == 3
    if feature_major:
        per_b = mla_o.shape[2] // tm
        mla_spec = pl.BlockSpec((None, half, tm), lambda i: (i // per_b, 0, lax.rem(i, per_b)))
    else:
        mla_spec = pl.BlockSpec((tm, half), lambda i: (i, 0))
    return pl.pallas_call(
        functools.partial(_out_proj_body, mla_feature_major=feature_major),
        grid=(n // tm,),
        in_specs=[mla_spec, pl.BlockSpec((tm, half), lambda i: (i, 0)),
                  pl.BlockSpec((tm, d), lambda i: (i, 0)), pl.BlockSpec((1, half), lambda i: (0, 0)),
                  pl.BlockSpec(w.shape, lambda i: (0, 0))],
        out_specs=pl.BlockSpec((tm, d), lambda i: (i, 0)),
        out_shape=jax.ShapeDtypeStruct((n, d), F32),
        compiler_params=_cparams("parallel"),
        name="out_proj",
    )(mla_o, gla_y, x, nm.reshape(1, half), w)


def _top_desc(s, n):
    vals = []
    w = s
    for _ in range(n):
        m = jnp.max(w, axis=0, keepdims=True)
        vals.append(m)
        w = jnp.where(w == m, NEG_INF, w)
    return vals


def _peer_route_body(h_ref, g_ref, wqt_ref, k1_ref, k2_ref, xnt_ref, s2_ref, e2_ref, thr_ref, e1z_ref):
    xnt = _rms(h_ref[...], g_ref[...]).T.astype(BF16)
    xnt_ref[...] = xnt
    qt = _dot(wqt_ref[...], xnt).astype(BF16)
    half = PEER_N_KEYS
    for h in range(PEER_HEADS):
        s1 = _dot(k1_ref[...], qt[(2 * h) * half:(2 * h + 1) * half])
        s2 = _dot(k2_ref[...], qt[(2 * h + 1) * half:(2 * h + 2) * half])
        v1 = _top_desc(s1, PEER_TOPK)
        v2 = _top_desc(s2, PEER_TOPK)
        pairs = [(a, b) for a in range(PEER_TOPK) for b in range(PEER_TOPK // (a + 1))]
        sums = {ab: v1[ab[0]] + v2[ab[1]] for ab in pairs}
        npad = -len(pairs) % 8
        stack = jnp.concatenate([sums[ab] for ab in pairs] + [jnp.full_like(v1[0], NEG_INF)] * npad, 0)
        tau = _top_desc(stack, PEER_TOPK)[-1]
        top = v1[0] + v2[0]
        zsum = jnp.sum(jnp.where(stack >= tau, jnp.exp(stack - top), 0.0), axis=0, keepdims=True)
        thr = jnp.full(s1.shape, POS_INF, F32)
        for a in range(PEER_TOPK):
            ta = jnp.full_like(tau, POS_INF)
            for b in range(PEER_TOPK // (a + 1)):
                ta = jnp.minimum(ta, jnp.where(sums[(a, b)] >= tau, v2[b], POS_INF))
            thr = jnp.where(s1 == v1[a], ta, thr)
        e2 = jnp.exp(s2 - v2[0])
        e1z = jnp.exp(s1 - v1[0]) / zsum
        for lg in range(thr.shape[1] // 128):
            lanes = slice(lg * 128, (lg + 1) * 128)
            s2_ref[h, lg] = s2[:, lanes]
            e2_ref[h, lg] = e2[:, lanes]
            thr_ref[h, lg] = thr[:, lanes]
            e1z_ref[h, lg] = e1z[:, lanes]


def _peer_route(h, g, wqt, k1, k2, tt):
    n, d = h.shape
    per_group = jax.ShapeDtypeStruct((PEER_HEADS, n // 128, PEER_N_KEYS, 128), F32)
    gspec = pl.BlockSpec((PEER_HEADS, tt // 128, PEER_N_KEYS, 128), lambda i: (0, i, 0, 0))
    return pl.pallas_call(
        _peer_route_body,
        grid=(n // tt,),
        in_specs=[pl.BlockSpec((tt, d), lambda i: (i, 0)), pl.BlockSpec((1, d), lambda i: (0, 0)),
                  pl.BlockSpec(wqt.shape, lambda i: (0, 0)), pl.BlockSpec(k1.shape, lambda i: (0, 0)),
                  pl.BlockSpec(k2.shape, lambda i: (0, 0))],
        out_specs=[pl.BlockSpec((d, tt), lambda i: (0, i)), gspec, gspec, gspec, gspec],
        out_shape=[jax.ShapeDtypeStruct((d, n), BF16), per_group, per_group, per_group, per_group],
        compiler_params=_cparams("parallel"),
        name="peer_route",
    )(h, g.reshape(1, d), wqt, k1, k2)


def _peer_dense_body(xnt_ref, u_ref, vt_ref, s2_ref, e2_ref, thr_ref, e1z_ref, yt_ref, act_a, act_b, ca_a, ca_b,
                     *, n_exp_tiles, n_pairs):
    g = pl.program_id(0)
    n_lg, te, _ = act_a.shape
    d = yt_ref.shape[0]
    groups = te // PEER_N_KEYS
    n_sub = PEER_N_KEYS // PEER_SUB
    u_rows, y_rows = te // n_lg, d // n_lg

    @pl.when(g == 0)
    def _():
        act_b[...] = jnp.zeros(act_b.shape, F32)
        ca_b[...] = jnp.zeros(ca_b.shape, BF16)

    @pl.when(lax.rem(jnp.clip(g - 2, 0, n_pairs - 1), n_exp_tiles) == 0)
    def _():
        yt_ref[...] = jnp.zeros(yt_ref.shape, F32)

    def tick_slice(lg, carry, act_w, act_r, ca_w, ca_r):
        r0 = pl.multiple_of(lg * u_rows, u_rows)
        act = _dot(u_ref[pl.ds(r0, u_rows), :], xnt_ref[...])
        act = 0.5 * act * (1.0 + lax.erf(act * (2.0 ** -0.5)))
        for j in range(n_lg):
            act_w[j, pl.ds(r0, u_rows), :] = act[:, j * 128:(j + 1) * 128]
        m0 = pl.multiple_of(lg * y_rows, y_rows)
        ca = jnp.concatenate([ca_r[j] for j in range(n_lg)], 1)
        yt_ref[pl.ds(m0, y_rows), :] += _dot(vt_ref[pl.ds(m0, y_rows), :], ca)
        for sq in range(n_sub):
            keys = slice(sq * PEER_SUB, (sq + 1) * PEER_SUB)
            coef = [jnp.zeros((PEER_SUB, 128), F32)] * groups
            for h in range(PEER_HEADS):
                s2b = s2_ref[h, lg, keys, :]
                e2b = e2_ref[h, lg, keys, :]
                for gi in range(groups):
                    thr = jnp.broadcast_to(thr_ref[h, lg, gi:gi + 1, :], (PEER_SUB, 128))
                    e1z = jnp.broadcast_to(e1z_ref[h, lg, gi:gi + 1, :], (PEER_SUB, 128))
                    coef[gi] = coef[gi] + jnp.where(s2b >= thr, e2b, 0.0) * e1z
            for gi in range(groups):
                rows = slice(gi * PEER_N_KEYS + sq * PEER_SUB, gi * PEER_N_KEYS + (sq + 1) * PEER_SUB)
                ca_w[lg, rows, :] = (coef[gi] * act_r[lg, rows, :]).astype(BF16)
        return carry

    def tick(act_w, act_r, ca_w, ca_r):
        lax.fori_loop(0, n_lg, functools.partial(tick_slice, act_w=act_w, act_r=act_r, ca_w=ca_w, ca_r=ca_r), 0)

    pl.when(lax.rem(g, 2) == 0)(lambda: tick(act_a, act_b, ca_a, ca_b))
    pl.when(lax.rem(g, 2) == 1)(lambda: tick(act_b, act_a, ca_b, ca_a))


def _peer_dense(xnt, u, vt, s2, e2, thr, e1z, tt, te):
    d, n = xnt.shape
    ne = u.shape[0] // te
    n_pairs = (n // tt) * ne
    gpt = te // PEER_N_KEYS
    pair = lambda g, lag: jnp.clip(g - lag, 0, n_pairs - 1)
    tok = lambda g, lag: pair(g, lag) // ne
    etile = lambda g, lag: lax.rem(pair(g, lag), ne)
    full = pl.BlockSpec((PEER_HEADS, tt // 128, PEER_N_KEYS, 128), lambda g: (0, tok(g, 1), 0, 0))
    part = pl.BlockSpec((PEER_HEADS, tt // 128, gpt, 128), lambda g: (0, tok(g, 1), etile(g, 1), 0))
    return pl.pallas_call(
        functools.partial(_peer_dense_body, n_exp_tiles=ne, n_pairs=n_pairs),
        grid=(n_pairs + 2,),
        in_specs=[pl.BlockSpec((d, tt), lambda g: (0, tok(g, 0))), pl.BlockSpec((te, d), lambda g: (etile(g, 0), 0)),
                  pl.BlockSpec((d, te), lambda g: (0, etile(g, 2))), full, full, part, part],
        out_specs=pl.BlockSpec((d, tt), lambda g: (0, tok(g, 2))),
        out_shape=jax.ShapeDtypeStruct((d, n), F32),
        scratch_shapes=[pltpu.VMEM((tt // 128, te, 128), F32)] * 2 + [pltpu.VMEM((tt // 128, te, 128), BF16)] * 2,
        compiler_params=_cparams("arbitrary"),
        name="peer_dense",
    )(xnt, u, vt, s2, e2, thr, e1z)


def _ple_body(h_ref, yt_ref, p_ref, g_ref, wg_ref, wp_ref, o_ref):
    h2 = h_ref[...] + yt_ref[...].T
    gate = jax.nn.sigmoid(_dot(_rms(h2, g_ref[...]).astype(BF16), wg_ref[...]))
    o_ref[...] = h2 + gate * _dot(p_ref[...].astype(BF16), wp_ref[...])


def _ple(h, yt, p, g, wg, wp, tm):
    n, d = h.shape
    return pl.pallas_call(
        _ple_body,
        grid=(n // tm,),
        in_specs=[pl.BlockSpec((tm, d), lambda i: (i, 0)), pl.BlockSpec((d, tm), lambda i: (0, i)),
                  pl.BlockSpec((tm, p.shape[1]), lambda i: (i, 0)), pl.BlockSpec((1, d), lambda i: (0, 0)),
                  pl.BlockSpec(wg.shape, lambda i: (0, 0)), pl.BlockSpec(wp.shape, lambda i: (0, 0))],
        out_specs=pl.BlockSpec((tm, d), lambda i: (i, 0)),
        out_shape=jax.ShapeDtypeStruct((n, d), F32),
        compiler_params=_cparams("parallel"),
        name="ple",
    )(h, yt, p, g.reshape(1, d), wg, wp)


def _rope_table(pos):
    half = MLA_ROPE // 2
    inv_freq = ROPE_THETA ** (-jnp.arange(half, dtype=F32) / half)
    ang = pos.astype(F32)[:, None] * inv_freq
    cos, sin = jnp.cos(ang), jnp.sin(ang)
    return jnp.concatenate([cos, cos, sin, sin], -1)


def _tile(n, t):
    return t if n % t == 0 else n


def _finish(x2, mla_o, gla_y, p2, w):
    n = x2.shape[0]
    h = _out_proj(mla_o, gla_y, x2, w["norm_mla_out"], w["w_out"], _tile(n, TM_OUT))
    xnt, s2, e2, thr, e1z = _peer_route(h, w["norm_ffn"], w["wqt"], w["k1"], w["k2"], _tile(n, TT_ROUTE))
    yt = _peer_dense(xnt, w["u"], w["vt"], s2, e2, thr, e1z, _tile(n, TT_DENSE), TE_DENSE)
    return _ple(h, yt, p2, w["norm_ple"], w["w_ple_gate"], w["w_ple_proj"], _tile(n, TM_PLE))


def kernel(x_prompt, x_sample, cache_kv_latent, cache_k_rope, cache_k_inv_rms, state_gla, page_table, p_prompt,
           p_sample, norm_mix, w_in, norm_q_lat, w_uq, norm_kv_lat, w_uk, w_uv, qk_gain_q, qk_gain_k, norm_mla_out,
           w_gla_a, b_gla_a, norm_gla_out, w_out, norm_ffn, peer_w_q, peer_keys1, peer_keys2, peer_u, peer_v,
           norm_ple, w_ple_gate, w_ple_proj):
    depth = w_in.shape[0]
    b, s, d = x_prompt.shape
    bs, t_new, _ = x_sample.shape
    n_pages = page_table.shape[1]
    page = cache_kv_latent.shape[2]
    past_len = n_pages * page
    tab_p = _rope_table(jnp.arange(s))
    tab_s = _rope_table(past_len + jnp.arange(t_new))
    hp, hs = x_prompt.reshape(b * s, d), x_sample.reshape(bs * t_new, d)
    outs = [[] for _ in range(8)]
    for i in range(depth):
        wi = w_in[i]
        cuts = [0, 512, 768, 832, 1344, 1856, 2880, 3904, 3920]
        cq_w, ckv_w, kr_w, gq_w, gk_w, gv_w, r_w, alr_w = [wi[:, cuts[j]:cuts[j + 1]] for j in range(8)]
        w_in_ext = jnp.concatenate(
            [cq_w, gq_w, gk_w, ckv_w, kr_w, _rot_cols(kr_w), alr_w,
             jnp.zeros((d, 128 - GLA_GATE_RANK), F32), gv_w, r_w], 1).astype(BF16)
        wuq3 = w_uq[i].reshape(MLA_Q_RANK, MLA_HEADS, MLA_QK)
        wuq_ext = jnp.concatenate([wuq3, _rot_cols(wuq3[..., MLA_NOPE:])], -1).reshape(MLA_Q_RANK, -1).astype(BF16)
        gq, gk = qk_gain_q[i], qk_gain_k[i]
        gq_ext = jnp.concatenate([gq, _rot_gain(gq[MLA_NOPE:])]).reshape(1, -1)
        gk_ext = jnp.concatenate([gk, _rot_gain(gk[MLA_NOPE:])]).reshape(1, -1)
        wuk = w_uk[i].reshape(MLA_KV_RANK, -1).astype(BF16)
        wuv = w_uv[i].reshape(MLA_KV_RANK, -1).astype(BF16)
        wuk_t = jnp.transpose(w_uk[i], (1, 2, 0)).astype(BF16)
        wuv_h = jnp.transpose(w_uv[i], (1, 0, 2)).astype(BF16)
        wga = jnp.concatenate([w_gla_a[i], jnp.zeros((128 - GLA_GATE_RANK, w_gla_a.shape[2]), F32)], 0).astype(BF16)
        bga = b_gla_a[i].reshape(1, -1)
        gn = norm_gla_out[i].reshape(1, -1)
        w = {"norm_mla_out": norm_mla_out[i], "w_out": w_out[i].astype(BF16), "norm_ffn": norm_ffn[i],
             "wqt": peer_w_q[i].T.astype(BF16), "k1": peer_keys1[i].astype(BF16), "k2": peer_keys2[i].astype(BF16),
             "u": peer_u[i].astype(BF16), "vt": peer_v[i].T.astype(BF16), "norm_ple": norm_ple[i],
             "w_ple_gate": w_ple_gate[i].astype(BF16), "w_ple_proj": w_ple_proj[i].astype(BF16)}
        prep = (norm_q_lat[i], norm_kv_lat[i], gq_ext, gk_ext, wuq_ext, wuk)

        n = b * s
        z = _norm_matmul(hp, norm_mix[i], w_in_ext, _tile(n, TM_IN), TN_IN)
        tm = _tile(s, TM_PREP)
        qt, kt, vt, lat, kr, ir = _mla_prep(z, tab_p, s // tm, *prep, wuv, tm, sample=False)
        tq = _tile(s, FLASH_TILE)
        mla_o = _flash(qt.reshape(b, s, -1), kt.reshape(b, s, -1), vt, tq, FLASH_HEADS)
        gla_y, s_fin = _gla(z.reshape(b, s, Z_COLS), None, wga, bga, gn, GLA_CHUNK, GLA_SUB, b)
        hp_next = _finish(hp, mla_o, gla_y.reshape(n, -1), p_prompt[i].reshape(n, -1), w)
        outs[0].append(lat.reshape(b, s, -1))
        outs[1].append(kr.reshape(b, s, -1))
        outs[2].append(ir[:, :MLA_HEADS].reshape(b, s, -1))
        outs[3].append(s_fin)

        n = bs * t_new
        z = _norm_matmul(hs, norm_mix[i], w_in_ext, _tile(n, TM_IN), TN_IN)
        tm = _tile(n, TM_PREP)
        tab = jnp.tile(tab_s, (tm // t_new, 1))
        ql, qr, lat, kr, ir = _mla_prep(z, tab, 1, *prep, wuk_t, tm, sample=True)
        mla_o = _decode(page_table, ql.reshape(MLA_HEADS, bs, t_new, -1), qr.reshape(MLA_HEADS, bs, t_new, -1),
                        lat.reshape(bs, t_new, -1), kr.reshape(bs, t_new, -1), ir.reshape(bs, t_new, -1), wuv_h,
                        cache_kv_latent[i], jnp.transpose(cache_k_rope[i], (0, 2, 1)),
                        jnp.transpose(cache_k_inv_rms[i], (0, 2, 1)), _tile(n_pages, DECODE_PAGES))
        gla_y, s_new = _gla(z.reshape(bs, t_new, Z_COLS), state_gla[i], wga, bga, gn, t_new, t_new,
                            _tile(bs, GLA_SAMPLE_ROWS))
        hs_next = _finish(hs, mla_o.reshape(n, -1), gla_y.reshape(n, -1), p_sample[i].reshape(n, -1), w)
        outs[4].append(lat.reshape(bs, t_new, -1))
        outs[5].append(kr.reshape(bs, t_new, -1))
        outs[6].append(ir[:, :MLA_HEADS].reshape(bs, t_new, -1))
        outs[7].append(s_new)
        hp, hs = hp_next, hs_next
    return (hp.reshape(b, s, d), hs.reshape(bs, t_new, d)) + tuple(jnp.stack(o, 0) for o in outs)
```

```python
import functools

import jax
import jax.numpy as jnp
from jax import lax
from jax.experimental import pallas as pl
from jax.experimental.pallas import tpu as pltpu

F32 = jnp.float32
BF16 = jnp.bfloat16
EPS = 1e-6
NEG_INF = float("-inf")
POS_INF = float("inf")

MLA_HEADS = 8
MLA_NOPE = 128
MLA_ROPE = 64
MLA_QK = MLA_NOPE + MLA_ROPE
MLA_V = 128
MLA_Q_RANK = 512
MLA_KV_RANK = 256
MLA_SCALE = MLA_QK ** -0.5
LOG2E = 1.4426950408889634
MLA_HEAD_PAD = 256
ROPE_THETA = 10000.0
GLA_HEADS = 4
GLA_DK = 128
GLA_DV = 256
GLA_GATE_RANK = 16
GLA_TAU = 16.0
GLA_CHUNK = 64
GLA_SUB = 16
PEER_HEADS = 8
PEER_N_KEYS = 128
PEER_TOPK = 16
PEER_SUB = 32
PLE_DIM = 256

Z_COLS = 4096
ZB_CQ = (512, 0)
ZB_GQ = (512, 1)
ZB_GK = (512, 2)
ZB_CKV = (256, 6)
ZB_KR = (128, 14)
ZB_ALR = (128, 15)
ZB_GV = (1024, 2)
ZB_R = (1024, 3)

V7X_VMEM_LIMIT = 56 * 1024 * 1024

TM_IN, TN_IN = 1024, 1024
TM_PREP = 256
FLASH_TILE = 512
FLASH_HEADS = 8
FLASH_SUB = 256
DECODE_PAGES = 64
DECODE_PARTS = 8
GLA_SAMPLE_ROWS = 8
GLA_ROWS_PER_ITER = 2
TM_OUT = 256
TT_ROUTE = 256
TT_DENSE, TE_DENSE = 512, 1024
TM_PLE = 256


def _cparams(*sem):
    return pltpu.CompilerParams(dimension_semantics=sem, vmem_limit_bytes=V7X_VMEM_LIMIT)


def _dot(a, b, **kw):
    return jnp.dot(a, b, preferred_element_type=F32, **kw)


def _dot_nt(a, b):
    return lax.dot_general(a, b, (((1,), (1,)), ((), ())), preferred_element_type=F32)


def _dot_tn(a, b, **kw):
    return lax.dot_general(a, b, (((0,), (0,)), ((), ())), preferred_element_type=F32, **kw)


def _rms(x, g):
    return x * lax.rsqrt(jnp.mean(x * x, -1, keepdims=True) + EPS) * g


def _rot_cols(w):
    h = w.shape[-1] // 2
    return jnp.concatenate([-w[..., h:], w[..., :h]], -1)


def _rot_gain(g):
    h = g.shape[-1] // 2
    return jnp.concatenate([g[..., h:], g[..., :h]], -1)


def _norm_matmul_body(x_ref, g_ref, w_ref, o_ref, xn_sc):
    @pl.when(pl.program_id(1) == 0)
    def _():
        xn_sc[...] = _rms(x_ref[...], g_ref[...]).astype(BF16)

    o_ref[...] = _dot(xn_sc[...], w_ref[...])


def _norm_matmul(x, g, w, tm, tn):
    n, d = x.shape
    nc = w.shape[1]
    return pl.pallas_call(
        _norm_matmul_body,
        grid=(n // tm, nc // tn),
        in_specs=[pl.BlockSpec((tm, d), lambda i, j: (i, 0)),
                  pl.BlockSpec((1, d), lambda i, j: (0, 0)),
                  pl.BlockSpec((d, tn), lambda i, j: (0, j))],
        out_specs=pl.BlockSpec((tm, tn), lambda i, j: (i, j)),
        out_shape=jax.ShapeDtypeStruct((n, nc), F32),
        scratch_shapes=[pltpu.VMEM((tm, d), BF16)],
        compiler_params=_cparams("parallel", "arbitrary"),
        name="in_proj",
    )(x, g.reshape(1, d), w)


def _mla_prep_body(cq_ref, ckv_ref, krb_ref, tab_ref, nq_ref, nkv_ref, gq_ref, gk_ref, wuq_ref, wuk_ref, wx_ref,
                   *outs, sample):
    if sample:
        ql_ref, qr_ref, lat_ref, kr_ref, ir_ref = outs
    else:
        qt_ref, kt_ref, vt_ref, lat_ref, kr_ref, ir_ref = outs
    tab = tab_ref[...]
    gq = gq_ref[...]
    gk = gk_ref[...]
    cqn = _rms(cq_ref[...], nq_ref[...]).astype(BF16)
    qraw = _dot(cqn, wuq_ref[...])
    c = _rms(ckv_ref[...], nkv_ref[...])
    lat_ref[...] = c
    cb = c.astype(BF16)
    kn = _dot(cb, wuk_ref[...])
    if not sample:
        vt_ref[...] = _dot(cb, wx_ref[...]).T.astype(BF16)
    krb = krb_ref[...]
    lane = lax.broadcasted_iota(jnp.int32, krb.shape, 1)
    ss_kr = jnp.sum(jnp.where(lane < MLA_ROPE, krb * krb, 0.0), -1, keepdims=True)
    t = krb * gk[:, MLA_NOPE:] * tab
    kr2 = t + pltpu.roll(t, MLA_ROPE, 1)
    kr_ref[...] = kr2[:, :MLA_ROPE]
    krz = jnp.where(lane < MLA_ROPE, kr2, 0.0)
    lane_q = lax.broadcasted_iota(jnp.int32, (krb.shape[0], MLA_HEAD_PAD), 1)
    ir_acc = jnp.zeros(krb.shape, F32)
    for h in range(MLA_HEADS):
        knh = kn[:, h * MLA_NOPE:(h + 1) * MLA_NOPE]
        ss = jnp.sum(knh * knh, -1, keepdims=True) + ss_kr
        inv = lax.rsqrt(ss / MLA_QK + EPS)
        ir_acc = jnp.where(lane == h, inv, ir_acc)
        qh = qraw[:, h * MLA_HEAD_PAD:(h + 1) * MLA_HEAD_PAD]
        ssq = jnp.sum(jnp.where(lane_q < MLA_QK, qh * qh, 0.0), -1, keepdims=True)
        qinv = lax.rsqrt(ssq / MLA_QK + EPS)
        qn = qh[:, :MLA_NOPE] * qinv * gq[:, :MLA_NOPE]
        tq = qh[:, MLA_NOPE:] * qinv * gq[:, MLA_NOPE:] * tab
        qr2 = tq + pltpu.roll(tq, MLA_ROPE, 1)
        if sample:
            qg = (qn * gk[:, :MLA_NOPE]).astype(BF16)
            ql_ref[h] = _dot(qg, wx_ref[h]) * MLA_SCALE
            qr_ref[h] = qr2[:, :MLA_ROPE] * MLA_SCALE
        else:
            lo = h * MLA_HEAD_PAD
            qt_ref[:, lo:lo + MLA_NOPE] = (qn * (MLA_SCALE * LOG2E)).astype(BF16)
            qt_ref[:, lo + MLA_NOPE:lo + MLA_HEAD_PAD] = (qr2 * (MLA_SCALE * LOG2E)).astype(BF16)
            kt_ref[:, lo:lo + MLA_NOPE] = (knh * gk[:, :MLA_NOPE] * inv).astype(BF16)
            kt_ref[:, lo + MLA_NOPE:lo + MLA_HEAD_PAD] = (krz * inv).astype(BF16)
    ir_ref[...] = ir_acc


def _mla_prep(z, tab, tab_blocks, nq, nkv, gq_ext, gk_ext, wuq_ext, wuk, wx, tm, sample):
    n = z.shape[0]
    hp = MLA_HEADS * MLA_HEAD_PAD

    def const(shape):
        return pl.BlockSpec(shape, lambda i: (0,) * len(shape))

    in_specs = [pl.BlockSpec((tm, ZB_CQ[0]), lambda i: (i, ZB_CQ[1])),
                pl.BlockSpec((tm, ZB_CKV[0]), lambda i: (i, ZB_CKV[1])),
                pl.BlockSpec((tm, ZB_KR[0]), lambda i: (i, ZB_KR[1])),
                pl.BlockSpec((tm, 128), lambda i: (i % tab_blocks, 0)),
                const((1, MLA_Q_RANK)), const((1, MLA_KV_RANK)), const((1, MLA_HEAD_PAD)), const((1, MLA_HEAD_PAD)),
                const(wuq_ext.shape), const(wuk.shape), const(wx.shape)]
    row = lambda w: pl.BlockSpec((tm, w), lambda i: (i, 0))
    tail_specs = [row(MLA_KV_RANK), row(MLA_ROPE), row(128)]
    tail_shapes = [jax.ShapeDtypeStruct((n, MLA_KV_RANK), F32), jax.ShapeDtypeStruct((n, MLA_ROPE), F32),
                   jax.ShapeDtypeStruct((n, 128), F32)]
    if sample:
        head = lambda w: pl.BlockSpec((MLA_HEADS, tm, w), lambda i: (0, i, 0))
        out_specs = [head(MLA_KV_RANK), head(MLA_ROPE)] + tail_specs
        out_shape = [jax.ShapeDtypeStruct((MLA_HEADS, n, MLA_KV_RANK), F32),
                     jax.ShapeDtypeStruct((MLA_HEADS, n, MLA_ROPE), F32)] + tail_shapes
    else:
        out_specs = [row(hp), row(hp), pl.BlockSpec((MLA_HEADS * MLA_V, tm), lambda i: (0, i))] + tail_specs
        out_shape = [jax.ShapeDtypeStruct((n, hp), BF16), jax.ShapeDtypeStruct((n, hp), BF16),
                     jax.ShapeDtypeStruct((MLA_HEADS * MLA_V, n), BF16)] + tail_shapes
    return pl.pallas_call(
        functools.partial(_mla_prep_body, sample=sample),
        grid=(n // tm,),
        in_specs=in_specs, out_specs=out_specs, out_shape=out_shape,
        compiler_params=_cparams("parallel"),
        name="mla_prep_sample" if sample else "mla_prep_prompt",
    )(z, z, z, tab, nq.reshape(1, -1), nkv.reshape(1, -1), gq_ext, gk_ext, wuq_ext, wuk, wx)


def _flash_body(q_ref, k_ref, vt_ref, o_ref, m_sc, l_sc, acc_sc, *, heads, sub):
    i = pl.program_id(2)
    j = pl.program_id(3)

    @pl.when(j == 0)
    def _():
        m_sc[...] = jnp.full(m_sc.shape, NEG_INF, F32)
        l_sc[...] = jnp.zeros(l_sc.shape, F32)
        acc_sc[...] = jnp.zeros(acc_sc.shape, F32)

    tile = q_ref.shape[0]

    def step(diagonal):
        def q_cols(c, carry):
            r0 = pl.multiple_of(c * sub, sub)
            cols = pl.ds(r0, sub)
            vrows = [slice(h * MLA_V, (h + 1) * MLA_V) for h in range(heads)]
            prev = [(m_sc[h, :, cols], l_sc[h, :, cols], acc_sc[vrows[h], cols]) for h in range(heads)]
            new = []

            def scores(h):
                qk = slice(h * MLA_HEAD_PAD, (h + 1) * MLA_HEAD_PAD)
                return _dot_nt(k_ref[:, qk], q_ref[cols, qk])

            ahead = 3
            pending = [scores(h) for h in range(min(ahead, heads))]
            for h in range(heads):
                s = pending.pop(0)
                if diagonal:
                    keep = (lax.broadcasted_iota(jnp.int32, s.shape, 0)
                            <= r0 + lax.broadcasted_iota(jnp.int32, s.shape, 1))
                    s = jnp.where(keep, s, NEG_INF)
                m_prev, l_prev, acc_prev = prev[h]
                m_new = jnp.maximum(m_prev, jnp.max(s, 0, keepdims=True))
                alpha = jnp.exp2(m_prev - m_new)
                p = jnp.exp2(s - m_new)
                l_new = alpha * l_prev + jnp.sum(p, 0, keepdims=True)
                acc = alpha * acc_prev + _dot(vt_ref[vrows[h], :], p.astype(BF16))
                new.append((m_new, l_new, acc))
                if h + ahead < heads:
                    pending.append(scores(h + ahead))
            for h, (m_new, l_new, acc) in enumerate(new):
                if diagonal:
                    o_ref[vrows[h], cols] = acc / l_new
                else:
                    m_sc[h, :, cols] = m_new
                    l_sc[h, :, cols] = l_new
                    acc_sc[vrows[h], cols] = acc
            return carry

        lax.fori_loop(0, tile // sub, q_cols, 0)

    pl.when(j < i)(lambda: step(False))
    pl.when(j == i)(lambda: step(True))


def _flash(qt, kt, vt, tile, heads):
    b, s, _ = qt.shape
    nt = s // tile
    return pl.pallas_call(
        functools.partial(_flash_body, heads=heads, sub=min(FLASH_SUB, tile)),
        grid=(b, MLA_HEADS // heads, nt, nt),
        in_specs=[pl.BlockSpec((None, tile, heads * MLA_HEAD_PAD), lambda bb, h, i, j: (bb, i, h)),
                  pl.BlockSpec((None, tile, heads * MLA_HEAD_PAD), lambda bb, h, i, j: (bb, jnp.minimum(j, i), h)),
                  pl.BlockSpec((heads * MLA_V, tile), lambda bb, h, i, j: (h, bb * nt + jnp.minimum(j, i)))],
        out_specs=pl.BlockSpec((None, heads * MLA_V, tile), lambda bb, h, i, j: (bb, h, i)),
        out_shape=jax.ShapeDtypeStruct((b, MLA_HEADS * MLA_V, s), F32),
        scratch_shapes=[pltpu.VMEM((heads, 1, tile), F32), pltpu.VMEM((heads, 1, tile), F32),
                        pltpu.VMEM((heads * MLA_V, tile), F32)],
        compiler_params=_cparams("parallel", "parallel", "parallel", "arbitrary"),
        name="mla_prompt_attn",
    )(qt, kt, vt)


def _decode_body(pt_ref, ql_ref, qr_ref, cn_ref, krn_ref, irn_ref, wuv_ref, lat_pool, krt_pool, irt_pool,
                 o_ref, m_sc, l_sc, acc_sc, lat_buf, krt_buf, irt_buf, sems, *, pages, n_chunks, t_new):
    rows = MLA_HEADS * t_new
    page = lat_buf.shape[1] // pages
    t = pl.program_id(0)
    last = pl.num_programs(0) - 1
    c = lax.rem(t, n_chunks)
    slot = lax.rem(t, 2)

    def chunk_copies(step, buf_slot):
        copies = []
        for p in range(pages):
            pg = pt_ref[step * pages + p]
            keys = pl.ds(p * page, page)
            copies.append(pltpu.make_async_copy(lat_pool.at[pg], lat_buf.at[buf_slot, keys, :], sems.at[buf_slot, 0]))
            copies.append(pltpu.make_async_copy(krt_pool.at[pg], krt_buf.at[buf_slot, p], sems.at[buf_slot, 1]))
            copies.append(pltpu.make_async_copy(irt_pool.at[pg], irt_buf.at[buf_slot, p], sems.at[buf_slot, 2]))
        return copies

    @pl.when(t == 0)
    def _():
        for cp in chunk_copies(0, 0):
            cp.start()

    @pl.when(c == 0)
    def _():
        m_sc[...] = jnp.full(m_sc.shape, NEG_INF, F32)
        l_sc[...] = jnp.zeros(l_sc.shape, F32)
        acc_sc[...] = jnp.zeros(acc_sc.shape, F32)

    for cp in chunk_copies(t, slot):
        cp.wait()
    nxt = jnp.minimum(t + 1, last)
    for cp in chunk_copies(nxt, 1 - slot):
        cp.start()

    ql = ql_ref[...].reshape(rows, MLA_KV_RANK).astype(BF16)
    qr = qr_ref[...].reshape(rows, MLA_ROPE).astype(BF16)

    def attend(parts, mask=None):
        ss = []
        for lat_b, krt_b, irt in parts:
            s = _dot_nt(ql, lat_b) + _dot(qr, krt_b)
            s = (s.reshape(MLA_HEADS, t_new, -1) * irt[:, None, :]).reshape(rows, -1)
            ss.append(s if mask is None else jnp.where(mask, s, NEG_INF))
        m_prev = m_sc[...]
        m_new = functools.reduce(jnp.maximum, [jnp.max(s, -1, keepdims=True) for s in ss], m_prev)
        alpha = jnp.exp(m_prev - m_new)
        l = alpha * l_sc[...]
        acc = alpha * acc_sc[...]
        for s, (lat_b, _, _) in zip(ss, parts):
            p = jnp.exp(s - m_new)
            l = l + jnp.sum(p, -1, keepdims=True)
            acc = acc + _dot(p.astype(BF16), lat_b)
        l_sc[...] = l
        acc_sc[...] = acc
        m_sc[...] = m_new

    n_parts = min(DECODE_PARTS, pages)
    pp = pages // n_parts
    attend([(lat_buf[slot, i * pp * page:(i + 1) * pp * page, :].astype(BF16),
             jnp.concatenate([krt_buf[slot, p] for p in range(i * pp, (i + 1) * pp)], 1).astype(BF16),
             jnp.concatenate([irt_buf[slot, p] for p in range(i * pp, (i + 1) * pp)], 1)) for i in range(n_parts)])

    @pl.when(c == n_chunks - 1)
    def _():
        pad = 128 - t_new
        lb = jnp.concatenate([cn_ref[...], jnp.zeros((pad, MLA_KV_RANK), F32)], 0).astype(BF16)
        kb = jnp.concatenate([krn_ref[...], jnp.zeros((pad, MLA_ROPE), F32)], 0).T.astype(BF16)
        ib = jnp.concatenate([irn_ref[...], jnp.zeros((pad, 128), F32)], 0).T[:MLA_HEADS]
        tq = lax.rem(lax.broadcasted_iota(jnp.int32, (rows, 128), 0), t_new)
        key = lax.broadcasted_iota(jnp.int32, (rows, 128), 1)
        attend([(lb, kb, ib)], mask=key <= tq)
        o_lat = acc_sc[...] / l_sc[...]
        for h in range(MLA_HEADS):
            o_ref[:, h * MLA_V:(h + 1) * MLA_V] = _dot(o_lat[h * t_new:(h + 1) * t_new].astype(BF16), wuv_ref[h])

    @pl.when(t == last)
    def _():
        for cp in chunk_copies(last, 1 - slot):
            cp.wait()


def _decode(page_table, ql, qr, c_new, kr_new, ir_new, wuv_h, lat_pool, krt_pool, irt_pool, pages):
    bs, n_pages = page_table.shape
    t_new = c_new.shape[1]
    page = lat_pool.shape[1]
    n_chunks = n_pages // pages
    keys = pages * page
    rows = MLA_HEADS * t_new
    per_b = lambda shape: pl.BlockSpec((None,) + shape, lambda t, pt: (t // n_chunks,) + (0,) * len(shape))
    hbm = pl.BlockSpec(memory_space=pl.ANY)
    in_specs = [pl.BlockSpec((MLA_HEADS, None, t_new, MLA_KV_RANK), lambda t, pt: (0, t // n_chunks, 0, 0)),
                pl.BlockSpec((MLA_HEADS, None, t_new, MLA_ROPE), lambda t, pt: (0, t // n_chunks, 0, 0)),
                per_b((t_new, MLA_KV_RANK)), per_b((t_new, MLA_ROPE)), per_b((t_new, 128)),
                pl.BlockSpec(wuv_h.shape, lambda t, pt: (0, 0, 0)), hbm, hbm, hbm]
    grid_spec = pltpu.PrefetchScalarGridSpec(
        num_scalar_prefetch=1,
        grid=(bs * n_chunks,),
        in_specs=in_specs,
        out_specs=per_b((t_new, MLA_HEADS * MLA_V)),
        scratch_shapes=[pltpu.VMEM((rows, 1), F32), pltpu.VMEM((rows, 1), F32), pltpu.VMEM((rows, MLA_KV_RANK), F32),
                        pltpu.VMEM((2, keys, MLA_KV_RANK), F32), pltpu.VMEM((2, pages, MLA_ROPE, page), F32),
                        pltpu.VMEM((2, pages, MLA_HEADS, page), F32), pltpu.SemaphoreType.DMA((2, 3))],
    )
    return pl.pallas_call(
        functools.partial(_decode_body, pages=pages, n_chunks=n_chunks, t_new=t_new),
        grid_spec=grid_spec,
        out_shape=jax.ShapeDtypeStruct((bs, t_new, MLA_HEADS * MLA_V), F32),
        compiler_params=_cparams("arbitrary"),
        name="mla_sample_attn",
    )(page_table.reshape(-1), ql, qr, c_new, kr_new, ir_new, wuv_h, lat_pool, krt_pool, irt_pool)


def _gla_body(gq_ref, gk_ref, gv_ref, r_ref, alr_ref, wga_ref, bga_ref, gn_ref, *rest, chunk, sub, nb, carry):
    if carry:
        y_ref, s_out_ref = rest
        s_in_ref = s_out_ref

        @pl.when(pl.program_id(0) == 0)
        def _():
            s_out_ref[...] = jnp.zeros(s_out_ref.shape, F32)
    else:
        s_in_ref, y_ref, s_out_ref = rest
    row = lax.broadcasted_iota(jnp.int32, (chunk, chunk), 0)
    col = lax.broadcasted_iota(jnp.int32, (chunk, chunk), 1)
    tri = (col <= row).astype(F32)
    ones_cv = jnp.ones((chunk, GLA_DV), F32)
    brow = lax.broadcasted_iota(jnp.int32, (sub, chunk), 0)
    bcol = lax.broadcasted_iota(jnp.int32, (sub, chunk), 1)
    hi = lax.Precision.HIGHEST
    gn = gn_ref[...]

    def group(bis):
        chains = [(n, h) for n in range(len(bis)) for h in range(GLA_HEADS)]
        ks = lambda h: slice(h * GLA_DK, (h + 1) * GLA_DK)
        vs = lambda h: slice(h * GLA_DV, (h + 1) * GLA_DV)
        gate = []
        for bi in bis:
            a = _dot(alr_ref[bi].astype(BF16), wga_ref[...]) + bga_ref[...]
            gate.append((jnp.minimum(a, 0.0) - jnp.log1p(jnp.exp(-jnp.abs(a)))) / GLA_TAU)
        cum = [_dot(tri, g, precision=hi) for g in gate]
        q = {(n, h): gq_ref[bis[n], :, ks(h)] * GLA_DK ** -0.5 for n, h in chains}
        k = {(n, h): gk_ref[bis[n], :, ks(h)] for n, h in chains}
        vb = {(n, h): gv_ref[bis[n], :, vs(h)].astype(BF16) for n, h in chains}
        s0 = {(n, h): s_in_ref[bis[n], h] for n, h in chains}
        bc = {(n, h): cum[n][:, ks(h)] for n, h in chains}
        o = {ch: _dot((q[ch] * jnp.exp(bc[ch])).astype(BF16), s0[ch].astype(BF16)) for ch in chains}
        for n, h in chains:
            ch = (n, h)
            dec = jnp.exp(_dot_tn(gate[n][:, ks(h)], ones_cv, precision=hi))
            kd = (k[ch] * jnp.exp(bc[ch][chunk - 1:chunk] - bc[ch])).astype(BF16)
            s_out_ref[bis[n], h] = dec * s0[ch] + _dot_tn(kd, vb[ch])
        for blk in range(chunk // sub):
            r0 = blk * sub
            a_blk = {}
            for ch in chains:
                qb = q[ch][r0:r0 + sub]
                bb = bc[ch][r0:r0 + sub]
                ab = jnp.zeros((sub, chunk), F32)
                for s in range(sub):
                    e = jnp.exp(jnp.minimum(bb - bb[s:s + 1], 0.0))
                    cs = jnp.sum(qb * e * k[ch][r0 + s:r0 + s + 1], -1, keepdims=True)
                    ab = jnp.where(bcol == r0 + s, jnp.where(brow >= s, cs, 0.0), ab)
                if blk > 0:
                    b0 = bc[ch][r0 - 1:r0]
                    qs = (qb * jnp.exp(bb - b0)).astype(BF16)
                    ksc = (k[ch] * jnp.exp(jnp.minimum(b0 - bc[ch], 0.0))).astype(BF16)
                    ab = jnp.where(bcol < r0, _dot_nt(qs, ksc), ab)
                a_blk[ch] = ab
            for n, h in chains:
                ch = (n, h)
                ob = o[ch][r0:r0 + sub] + _dot(a_blk[ch].astype(BF16), vb[ch])
                rr = r_ref[bis[n], r0:r0 + sub, vs(h)]
                y_ref[bis[n], r0:r0 + sub, vs(h)] = _rms(ob, gn) * (rr * jax.nn.sigmoid(rr))

    if carry:
        group(list(range(nb)))
    else:
        def rows(i, carry_):
            group([i * GLA_ROWS_PER_ITER + j for j in range(GLA_ROWS_PER_ITER)])
            return carry_

        lax.fori_loop(0, nb // GLA_ROWS_PER_ITER, rows, 0)


def _gla(z3, state, wga, bga, gn, chunk, sub, nb):
    b, s, _ = z3.shape
    carry = state is None
    if carry:
        grid = (s // chunk,)
        zmap = lambda blk: (lambda c: (0, c, blk))
        smap = lambda c: (0, 0, 0, 0)
        nb = b
    else:
        grid = (b // nb,)
        zmap = lambda blk: (lambda i: (i, 0, blk))
        smap = lambda i: (i, 0, 0, 0)
    zspec = lambda wb: pl.BlockSpec((nb, chunk, wb[0]), zmap(wb[1]))
    const = lambda shape: pl.BlockSpec(shape, lambda i: (0,) * len(shape))
    sspec = pl.BlockSpec((nb, GLA_HEADS, GLA_DK, GLA_DV), smap)
    in_specs = [zspec(ZB_GQ), zspec(ZB_GK), zspec(ZB_GV), zspec(ZB_R), zspec(ZB_ALR),
                const(wga.shape), const(bga.shape), const(gn.shape)]
    args = [z3, z3, z3, z3, z3, wga, bga, gn]
    if not carry:
        in_specs.append(sspec)
        args.append(state)
    return pl.pallas_call(
        functools.partial(_gla_body, chunk=chunk, sub=sub, nb=nb, carry=carry),
        grid=grid,
        in_specs=in_specs,
        out_specs=[pl.BlockSpec((nb, chunk, GLA_HEADS * GLA_DV), zmap(0)), sspec],
        out_shape=[jax.ShapeDtypeStruct((b, s, GLA_HEADS * GLA_DV), F32),
                   jax.ShapeDtypeStruct((b, GLA_HEADS, GLA_DK, GLA_DV), F32)],
        compiler_params=_cparams("arbitrary"),
        name="gla_prompt" if carry else "gla_sample",
    )(*args)


def _out_proj_body(mla_ref, gla_ref, x_ref, nm_ref, w_ref, o_ref, *, mla_feature_major):
    mla = mla_ref[...].T if mla_feature_major else mla_ref[...]
    half = mla.shape[1]
    mn = _rms(mla, nm_ref[...]).astype(BF16)
    o_ref[...] = x_ref[...] + _dot(mn, w_ref[:half]) + _dot(gla_ref[...].astype(BF16), w_ref[half:])


def _out_proj(mla_o, gla_y, x, nm, w, tm):
    n, d = x.shape
    half = gla_y.shape[1]
    feature_major = mla_o.ndim == 3
    if feature_major:
        per_b = mla_o.shape[2] // tm
        mla_spec = pl.BlockSpec((None, half, tm), lambda i: (i // per_b, 0, lax.rem(i, per_b)))
    else:
        mla_spec = pl.BlockSpec((tm, half), lambda i: (i, 0))
    return pl.pallas_call(
        functools.partial(_out_proj_body, mla_feature_major=feature_major),
        grid=(n // tm,),
        in_specs=[mla_spec, pl.BlockSpec((tm, half), lambda i: (i, 0)),
                  pl.BlockSpec((tm, d), lambda i: (i, 0)), pl.BlockSpec((1, half), lambda i: (0, 0)),
                  pl.BlockSpec(w.shape, lambda i: (0, 0))],
        out_specs=pl.BlockSpec((tm, d), lambda i: (i, 0)),
        out_shape=jax.ShapeDtypeStruct((n, d), F32),
        compiler_params=_cparams("parallel"),
        name="out_proj",
    )(mla_o, gla_y, x, nm.reshape(1, half), w)


def _top_desc(s, n):
    vals = []
    w = s
    for _ in range(n):
        m = jnp.max(w, axis=0, keepdims=True)
        vals.append(m)
        w = jnp.where(w == m, NEG_INF, w)
    return vals


def _peer_route_body(h_ref, g_ref, wqt_ref, k1_ref, k2_ref, xnt_ref, s2_ref, e2_ref, thr_ref, e1z_ref):
    xnt = _rms(h_ref[...], g_ref[...]).T.astype(BF16)
    xnt_ref[...] = xnt
    qt = _dot(wqt_ref[...], xnt).astype(BF16)
    half = PEER_N_KEYS
    for h in range(PEER_HEADS):
        s1 = _dot(k1_ref[...], qt[(2 * h) * half:(2 * h + 1) * half])
        s2 = _dot(k2_ref[...], qt[(2 * h + 1) * half:(2 * h + 2) * half])
        v1 = _top_desc(s1, PEER_TOPK)
        v2 = _top_desc(s2, PEER_TOPK)
        pairs = [(a, b) for a in range(PEER_TOPK) for b in range(PEER_TOPK // (a + 1))]
        sums = {ab: v1[ab[0]] + v2[ab[1]] for ab in pairs}
        npad = -len(pairs) % 8
        stack = jnp.concatenate([sums[ab] for ab in pairs] + [jnp.full_like(v1[0], NEG_INF)] * npad, 0)
        tau = _top_desc(stack, PEER_TOPK)[-1]
        top = v1[0] + v2[0]
        zsum = jnp.sum(jnp.where(stack >= tau, jnp.exp(stack - top), 0.0), axis=0, keepdims=True)
        thr = jnp.full(s1.shape, POS_INF, F32)
        for a in range(PEER_TOPK):
            ta = jnp.full_like(tau, POS_INF)
            for b in range(PEER_TOPK // (a + 1)):
                ta = jnp.minimum(ta, jnp.where(sums[(a, b)] >= tau, v2[b], POS_INF))
            thr = jnp.where(s1 == v1[a], ta, thr)
        e2 = jnp.exp(s2 - v2[0])
        e1z = jnp.exp(s1 - v1[0]) / zsum
        for lg in range(thr.shape[1] // 128):
            lanes = slice(lg * 128, (lg + 1) * 128)
            s2_ref[h, lg] = s2[:, lanes]
            e2_ref[h, lg] = e2[:, lanes]
            thr_ref[h, lg] = thr[:, lanes]
            e1z_ref[h, lg] = e1z[:, lanes]


def _peer_route(h, g, wqt, k1, k2, tt):
    n, d = h.shape
    per_group = jax.ShapeDtypeStruct((PEER_HEADS, n // 128, PEER_N_KEYS, 128), F32)
    gspec = pl.BlockSpec((PEER_HEADS, tt // 128, PEER_N_KEYS, 128), lambda i: (0, i, 0, 0))
    return pl.pallas_call(
        _peer_route_body,
        grid=(n // tt,),
        in_specs=[pl.BlockSpec((tt, d), lambda i: (i, 0)), pl.BlockSpec((1, d), lambda i: (0, 0)),
                  pl.BlockSpec(wqt.shape, lambda i: (0, 0)), pl.BlockSpec(k1.shape, lambda i: (0, 0)),
                  pl.BlockSpec(k2.shape, lambda i: (0, 0))],
        out_specs=[pl.BlockSpec((d, tt), lambda i: (0, i)), gspec, gspec, gspec, gspec],
        out_shape=[jax.ShapeDtypeStruct((d, n), BF16), per_group, per_group, per_group, per_group],
        compiler_params=_cparams("parallel"),
        name="peer_route",
    )(h, g.reshape(1, d), wqt, k1, k2)


def _peer_dense_body(xnt_ref, u_ref, vt_ref, s2_ref, e2_ref, thr_ref, e1z_ref, yt_ref, act_sc, ca_sc):
    @pl.when(pl.program_id(1) == 0)
    def _():
        yt_ref[...] = jnp.zeros(yt_ref.shape, F32)

    n_lg, te, _ = act_sc.shape
    act = _dot(u_ref[...], xnt_ref[...])
    act = 0.5 * act * (1.0 + lax.erf(act * (2.0 ** -0.5)))
    for lg in range(n_lg):
        act_sc[lg] = act[:, lg * 128:(lg + 1) * 128]
    groups = te // PEER_N_KEYS
    n_sub = PEER_N_KEYS // PEER_SUB

    def gate_block(blk, carry):
        lg = blk // n_sub
        k0 = pl.multiple_of(lax.rem(blk, n_sub) * PEER_SUB, PEER_SUB)
        coef = [jnp.zeros((PEER_SUB, 128), F32)] * groups
        for h in range(PEER_HEADS):
            s2b = s2_ref[h, lg, pl.ds(k0, PEER_SUB), :]
            e2b = e2_ref[h, lg, pl.ds(k0, PEER_SUB), :]
            for gi in range(groups):
                thr = jnp.broadcast_to(thr_ref[h, lg, gi:gi + 1, :], (PEER_SUB, 128))
                e1z = jnp.broadcast_to(e1z_ref[h, lg, gi:gi + 1, :], (PEER_SUB, 128))
                coef[gi] = coef[gi] + jnp.where(s2b >= thr, e2b, 0.0) * e1z
        for gi in range(groups):
            rows = pl.ds(pl.multiple_of(gi * PEER_N_KEYS + k0, PEER_SUB), PEER_SUB)
            ca_sc[lg, rows, :] = (coef[gi] * act_sc[lg, rows, :]).astype(BF16)
        return carry

    lax.fori_loop(0, n_lg * n_sub, gate_block, 0)
    ca = jnp.concatenate([ca_sc[lg] for lg in range(n_lg)], 1)
    yt_ref[...] += _dot(vt_ref[...], ca)


def _peer_dense(xnt, u, vt, s2, e2, thr, e1z, tt, te):
    d, n = xnt.shape
    ne = u.shape[0] // te
    gpt = te // PEER_N_KEYS
    full = pl.BlockSpec((PEER_HEADS, tt // 128, PEER_N_KEYS, 128), lambda i, j: (0, i, 0, 0))
    part = pl.BlockSpec((PEER_HEADS, tt // 128, gpt, 128), lambda i, j: (0, i, j, 0))
    return pl.pallas_call(
        _peer_dense_body,
        grid=(n // tt, ne),
        in_specs=[pl.BlockSpec((d, tt), lambda i, j: (0, i)), pl.BlockSpec((te, d), lambda i, j: (j, 0)),
                  pl.BlockSpec((d, te), lambda i, j: (0, j)), full, full, part, part],
        out_specs=pl.BlockSpec((d, tt), lambda i, j: (0, i)),
        out_shape=jax.ShapeDtypeStruct((d, n), F32),
        scratch_shapes=[pltpu.VMEM((tt // 128, te, 128), F32), pltpu.VMEM((tt // 128, te, 128), BF16)],
        compiler_params=_cparams("parallel", "arbitrary"),
        name="peer_dense",
    )(xnt, u, vt, s2, e2, thr, e1z)


def _ple_body(h_ref, yt_ref, p_ref, g_ref, wg_ref, wp_ref, o_ref):
    h2 = h_ref[...] + yt_ref[...].T
    gate = jax.nn.sigmoid(_dot(_rms(h2, g_ref[...]).astype(BF16), wg_ref[...]))
    o_ref[...] = h2 + gate * _dot(p_ref[...].astype(BF16), wp_ref[...])


def _ple(h, yt, p, g, wg, wp, tm):
    n, d = h.shape
    return pl.pallas_call(
        _ple_body,
        grid=(n // tm,),
        in_specs=[pl.BlockSpec((tm, d), lambda i: (i, 0)), pl.BlockSpec((d, tm), lambda i: (0, i)),
                  pl.BlockSpec((tm, p.shape[1]), lambda i: (i, 0)), pl.BlockSpec((1, d), lambda i: (0, 0)),
                  pl.BlockSpec(wg.shape, lambda i: (0, 0)), pl.BlockSpec(wp.shape, lambda i: (0, 0))],
        out_specs=pl.BlockSpec((tm, d), lambda i: (i, 0)),
        out_shape=jax.ShapeDtypeStruct((n, d), F32),
        compiler_params=_cparams("parallel"),
        name="ple",
    )(h, yt, p, g.reshape(1, d), wg, wp)


def _rope_table(pos):
    half = MLA_ROPE // 2
    inv_freq = ROPE_THETA ** (-jnp.arange(half, dtype=F32) / half)
    ang = pos.astype(F32)[:, None] * inv_freq
    cos, sin = jnp.cos(ang), jnp.sin(ang)
    return jnp.concatenate([cos, cos, sin, sin], -1)


def _tile(n, t):
    return t if n % t == 0 else n


def _finish(x2, mla_o, gla_y, p2, w):
    n = x2.shape[0]
    h = _out_proj(mla_o, gla_y, x2, w["norm_mla_out"], w["w_out"], _tile(n, TM_OUT))
    xnt, s2, e2, thr, e1z = _peer_route(h, w["norm_ffn"], w["wqt"], w["k1"], w["k2"], _tile(n, TT_ROUTE))
    yt = _peer_dense(xnt, w["u"], w["vt"], s2, e2, thr, e1z, _tile(n, TT_DENSE), TE_DENSE)
    return _ple(h, yt, p2, w["norm_ple"], w["w_ple_gate"], w["w_ple_proj"], _tile(n, TM_PLE))


def kernel(x_prompt, x_sample, cache_kv_latent, cache_k_rope, cache_k_inv_rms, state_gla, page_table, p_prompt,
           p_sample, norm_mix, w_in, norm_q_lat, w_uq, norm_kv_lat, w_uk, w_uv, qk_gain_q, qk_gain_k, norm_mla_out,
           w_gla_a, b_gla_a, norm_gla_out, w_out, norm_ffn, peer_w_q, peer_keys1, peer_keys2, peer_u, peer_v,
           norm_ple, w_ple_gate, w_ple_proj):
    depth = w_in.shape[0]
    b, s, d = x_prompt.shape
    bs, t_new, _ = x_sample.shape
    n_pages = page_table.shape[1]
    page = cache_kv_latent.shape[2]
    past_len = n_pages * page
    tab_p = _rope_table(jnp.arange(s))
    tab_s = _rope_table(past_len + jnp.arange(t_new))
    hp, hs = x_prompt.reshape(b * s, d), x_sample.reshape(bs * t_new, d)
    outs = [[] for _ in range(8)]
    for i in range(depth):
        wi = w_in[i]
        cuts = [0, 512, 768, 832, 1344, 1856, 2880, 3904, 3920]
        cq_w, ckv_w, kr_w, gq_w, gk_w, gv_w, r_w, alr_w = [wi[:, cuts[j]:cuts[j + 1]] for j in range(8)]
        w_in_ext = jnp.concatenate(
            [cq_w, gq_w, gk_w, ckv_w, kr_w, _rot_cols(kr_w), alr_w,
             jnp.zeros((d, 128 - GLA_GATE_RANK), F32), gv_w, r_w], 1).astype(BF16)
        wuq3 = w_uq[i].reshape(MLA_Q_RANK, MLA_HEADS, MLA_QK)
        wuq_ext = jnp.concatenate([wuq3, _rot_cols(wuq3[..., MLA_NOPE:])], -1).reshape(MLA_Q_RANK, -1).astype(BF16)
        gq, gk = qk_gain_q[i], qk_gain_k[i]
        gq_ext = jnp.concatenate([gq, _rot_gain(gq[MLA_NOPE:])]).reshape(1, -1)
        gk_ext = jnp.concatenate([gk, _rot_gain(gk[MLA_NOPE:])]).reshape(1, -1)
        wuk = w_uk[i].reshape(MLA_KV_RANK, -1).astype(BF16)
        wuv = w_uv[i].reshape(MLA_KV_RANK, -1).astype(BF16)
        wuk_t = jnp.transpose(w_uk[i], (1, 2, 0)).astype(BF16)
        wuv_h = jnp.transpose(w_uv[i], (1, 0, 2)).astype(BF16)
        wga = jnp.concatenate([w_gla_a[i], jnp.zeros((128 - GLA_GATE_RANK, w_gla_a.shape[2]), F32)], 0).astype(BF16)
        bga = b_gla_a[i].reshape(1, -1)
        gn = norm_gla_out[i].reshape(1, -1)
        w = {"norm_mla_out": norm_mla_out[i], "w_out": w_out[i].astype(BF16), "norm_ffn": norm_ffn[i],
             "wqt": peer_w_q[i].T.astype(BF16), "k1": peer_keys1[i].astype(BF16), "k2": peer_keys2[i].astype(BF16),
             "u": peer_u[i].astype(BF16), "vt": peer_v[i].T.astype(BF16), "norm_ple": norm_ple[i],
             "w_ple_gate": w_ple_gate[i].astype(BF16), "w_ple_proj": w_ple_proj[i].astype(BF16)}
        prep = (norm_q_lat[i], norm_kv_lat[i], gq_ext, gk_ext, wuq_ext, wuk)

        n = b * s
        z = _norm_matmul(hp, norm_mix[i], w_in_ext, _tile(n, TM_IN), TN_IN)
        tm = _tile(s, TM_PREP)
        qt, kt, vt, lat, kr, ir = _mla_prep(z, tab_p, s // tm, *prep, wuv, tm, sample=False)
        tq = _tile(s, FLASH_TILE)
        mla_o = _flash(qt.reshape(b, s, -1), kt.reshape(b, s, -1), vt, tq, FLASH_HEADS)
        gla_y, s_fin = _gla(z.reshape(b, s, Z_COLS), None, wga, bga, gn, GLA_CHUNK, GLA_SUB, b)
        hp_next = _finish(hp, mla_o, gla_y.reshape(n, -1), p_prompt[i].reshape(n, -1), w)
        outs[0].append(lat.reshape(b, s, -1))
        outs[1].append(kr.reshape(b, s, -1))
        outs[2].append(ir[:, :MLA_HEADS].reshape(b, s, -1))
        outs[3].append(s_fin)

        n = bs * t_new
        z = _norm_matmul(hs, norm_mix[i], w_in_ext, _tile(n, TM_IN), TN_IN)
        tm = _tile(n, TM_PREP)
        tab = jnp.tile(tab_s, (tm // t_new, 1))
        ql, qr, lat, kr, ir = _mla_prep(z, tab, 1, *prep, wuk_t, tm, sample=True)
        mla_o = _decode(page_table, ql.reshape(MLA_HEADS, bs, t_new, -1), qr.reshape(MLA_HEADS, bs, t_new, -1),
                        lat.reshape(bs, t_new, -1), kr.reshape(bs, t_new, -1), ir.reshape(bs, t_new, -1), wuv_h,
                        cache_kv_latent[i], jnp.transpose(cache_k_rope[i], (0, 2, 1)),
                        jnp.transpose(cache_k_inv_rms[i], (0, 2, 1)), _tile(n_pages, DECODE_PAGES))
        gla_y, s_new = _gla(z.reshape(bs, t_new, Z_COLS), state_gla[i], wga, bga, gn, t_new, t_new,
                            _tile(bs, GLA_SAMPLE_ROWS))
        hs_next = _finish(hs, mla_o.reshape(n, -1), gla_y.reshape(n, -1), p_sample[i].reshape(n, -1), w)
        outs[4].append(lat.reshape(bs, t_new, -1))
        outs[5].append(kr.reshape(bs, t_new, -1))
        outs[6].append(ir[:, :MLA_HEADS].reshape(bs, t_new, -1))
        outs[7].append(s_new)
        hp, hs = hp_next, hs_next
    return (hp.reshape(b, s, d), hs.reshape(bs, t_new, d)) + tuple(jnp.stack(o, 0) for o in outs)
```

```python
import functools

import jax
import jax.numpy as jnp
from jax import lax
from jax.experimental import pallas as pl
from jax.experimental.pallas import tpu as pltpu

F32 = jnp.float32
BF16 = jnp.bfloat16
EPS = 1e-6
NEG_INF = float("-inf")
POS_INF = float("inf")

MLA_HEADS = 8
MLA_NOPE = 128
MLA_ROPE = 64
MLA_QK = MLA_NOPE + MLA_ROPE
MLA_V = 128
MLA_Q_RANK = 512
MLA_KV_RANK = 256
MLA_SCALE = MLA_QK ** -0.5
LOG2E = 1.4426950408889634
MLA_HEAD_PAD = 256
ROPE_THETA = 10000.0
GLA_HEADS = 4
GLA_DK = 128
GLA_DV = 256
GLA_GATE_RANK = 16
GLA_TAU = 16.0
GLA_CHUNK = 64
GLA_SUB = 16
PEER_HEADS = 8
PEER_N_KEYS = 128
PEER_TOPK = 16
PEER_SUB = 32
PLE_DIM = 256

Z_COLS = 4096
ZB_CQ = (512, 0)
ZB_GQ = (512, 1)
ZB_GK = (512, 2)
ZB_CKV = (256, 6)
ZB_KR = (128, 14)
ZB_ALR = (128, 15)
ZB_GV = (1024, 2)
ZB_R = (1024, 3)

V7X_VMEM_LIMIT = 56 * 1024 * 1024

TM_IN, TN_IN = 1024, 1024
TM_PREP = 256
FLASH_TILE = 512
FLASH_HEADS = 8
FLASH_SUB = 512
DECODE_PAGES = 64
DECODE_PARTS = 8
DECODE_SLOTS = 3
GLA_SAMPLE_ROWS = 8
GLA_ROWS_PER_ITER = 2
TM_OUT = 256
TT_ROUTE = 256
TT_DENSE, TE_DENSE = 512, 1024
TM_PLE = 256


def _cparams(*sem):
    return pltpu.CompilerParams(dimension_semantics=sem, vmem_limit_bytes=V7X_VMEM_LIMIT)


def _dot(a, b, **kw):
    return jnp.dot(a, b, preferred_element_type=F32, **kw)


def _dot_nt(a, b):
    return lax.dot_general(a, b, (((1,), (1,)), ((), ())), preferred_element_type=F32)


def _dot_tn(a, b, **kw):
    return lax.dot_general(a, b, (((0,), (0,)), ((), ())), preferred_element_type=F32, **kw)


def _rms(x, g):
    return x * lax.rsqrt(jnp.mean(x * x, -1, keepdims=True) + EPS) * g


def _rot_cols(w):
    h = w.shape[-1] // 2
    return jnp.concatenate([-w[..., h:], w[..., :h]], -1)


def _rot_gain(g):
    h = g.shape[-1] // 2
    return jnp.concatenate([g[..., h:], g[..., :h]], -1)


def _norm_matmul_body(x_ref, g_ref, w_ref, o_ref, xn_sc):
    @pl.when(pl.program_id(1) == 0)
    def _():
        xn_sc[...] = _rms(x_ref[...], g_ref[...]).astype(BF16)

    o_ref[...] = _dot(xn_sc[...], w_ref[...])


def _norm_matmul(x, g, w, tm, tn):
    n, d = x.shape
    nc = w.shape[1]
    return pl.pallas_call(
        _norm_matmul_body,
        grid=(n // tm, nc // tn),
        in_specs=[pl.BlockSpec((tm, d), lambda i, j: (i, 0)),
                  pl.BlockSpec((1, d), lambda i, j: (0, 0)),
                  pl.BlockSpec((d, tn), lambda i, j: (0, j))],
        out_specs=pl.BlockSpec((tm, tn), lambda i, j: (i, j)),
        out_shape=jax.ShapeDtypeStruct((n, nc), F32),
        scratch_shapes=[pltpu.VMEM((tm, d), BF16)],
        compiler_params=_cparams("parallel", "arbitrary"),
        name="in_proj",
    )(x, g.reshape(1, d), w)


def _mla_prep_body(cq_ref, ckv_ref, krb_ref, tab_ref, nq_ref, nkv_ref, gq_ref, gk_ref, wuq_ref, wuk_ref, wx_ref,
                   *outs, sample):
    if sample:
        ql_ref, qr_ref, lat_ref, kr_ref, ir_ref = outs
    else:
        qt_ref, kt_ref, vt_ref, lat_ref, kr_ref, ir_ref = outs
    tab = tab_ref[...]
    gq = gq_ref[...]
    gk = gk_ref[...]
    cqn = _rms(cq_ref[...], nq_ref[...]).astype(BF16)
    qraw = _dot(cqn, wuq_ref[...])
    c = _rms(ckv_ref[...], nkv_ref[...])
    lat_ref[...] = c
    cb = c.astype(BF16)
    kn = _dot(cb, wuk_ref[...])
    if not sample:
        vt_ref[...] = _dot(cb, wx_ref[...]).T.astype(BF16)
    krb = krb_ref[...]
    lane = lax.broadcasted_iota(jnp.int32, krb.shape, 1)
    ss_kr = jnp.sum(jnp.where(lane < MLA_ROPE, krb * krb, 0.0), -1, keepdims=True)
    t = krb * gk[:, MLA_NOPE:] * tab
    kr2 = t + pltpu.roll(t, MLA_ROPE, 1)
    kr_ref[...] = kr2[:, :MLA_ROPE]
    krz = jnp.where(lane < MLA_ROPE, kr2, 0.0)
    lane_q = lax.broadcasted_iota(jnp.int32, (krb.shape[0], MLA_HEAD_PAD), 1)
    ir_acc = jnp.zeros(krb.shape, F32)
    for h in range(MLA_HEADS):
        knh = kn[:, h * MLA_NOPE:(h + 1) * MLA_NOPE]
        ss = jnp.sum(knh * knh, -1, keepdims=True) + ss_kr
        inv = lax.rsqrt(ss / MLA_QK + EPS)
        ir_acc = jnp.where(lane == h, inv, ir_acc)
        qh = qraw[:, h * MLA_HEAD_PAD:(h + 1) * MLA_HEAD_PAD]
        ssq = jnp.sum(jnp.where(lane_q < MLA_QK, qh * qh, 0.0), -1, keepdims=True)
        qinv = lax.rsqrt(ssq / MLA_QK + EPS)
        qn = qh[:, :MLA_NOPE] * qinv * gq[:, :MLA_NOPE]
        tq = qh[:, MLA_NOPE:] * qinv * gq[:, MLA_NOPE:] * tab
        qr2 = tq + pltpu.roll(tq, MLA_ROPE, 1)
        if sample:
            qg = (qn * gk[:, :MLA_NOPE]).astype(BF16)
            ql_ref[h] = _dot(qg, wx_ref[h]) * MLA_SCALE
            qr_ref[h] = qr2[:, :MLA_ROPE] * MLA_SCALE
        else:
            lo = h * MLA_HEAD_PAD
            qt_ref[:, lo:lo + MLA_NOPE] = (qn * (MLA_SCALE * LOG2E)).astype(BF16)
            qt_ref[:, lo + MLA_NOPE:lo + MLA_HEAD_PAD] = (qr2 * (MLA_SCALE * LOG2E)).astype(BF16)
            kt_ref[:, lo:lo + MLA_NOPE] = (knh * gk[:, :MLA_NOPE] * inv).astype(BF16)
            kt_ref[:, lo + MLA_NOPE:lo + MLA_HEAD_PAD] = (krz * inv).astype(BF16)
    ir_ref[...] = ir_acc


def _mla_prep(z, tab, tab_blocks, nq, nkv, gq_ext, gk_ext, wuq_ext, wuk, wx, tm, sample):
    n = z.shape[0]
    hp = MLA_HEADS * MLA_HEAD_PAD

    def const(shape):
        return pl.BlockSpec(shape, lambda i: (0,) * len(shape))

    in_specs = [pl.BlockSpec((tm, ZB_CQ[0]), lambda i: (i, ZB_CQ[1])),
                pl.BlockSpec((tm, ZB_CKV[0]), lambda i: (i, ZB_CKV[1])),
                pl.BlockSpec((tm, ZB_KR[0]), lambda i: (i, ZB_KR[1])),
                pl.BlockSpec((tm, 128), lambda i: (i % tab_blocks, 0)),
                const((1, MLA_Q_RANK)), const((1, MLA_KV_RANK)), const((1, MLA_HEAD_PAD)), const((1, MLA_HEAD_PAD)),
                const(wuq_ext.shape), const(wuk.shape), const(wx.shape)]
    row = lambda w: pl.BlockSpec((tm, w), lambda i: (i, 0))
    tail_specs = [row(MLA_KV_RANK), row(MLA_ROPE), row(128)]
    tail_shapes = [jax.ShapeDtypeStruct((n, MLA_KV_RANK), F32), jax.ShapeDtypeStruct((n, MLA_ROPE), F32),
                   jax.ShapeDtypeStruct((n, 128), F32)]
    if sample:
        head = lambda w: pl.BlockSpec((MLA_HEADS, tm, w), lambda i: (0, i, 0))
        out_specs = [head(MLA_KV_RANK), head(MLA_ROPE)] + tail_specs
        out_shape = [jax.ShapeDtypeStruct((MLA_HEADS, n, MLA_KV_RANK), F32),
                     jax.ShapeDtypeStruct((MLA_HEADS, n, MLA_ROPE), F32)] + tail_shapes
    else:
        out_specs = [row(hp), row(hp), pl.BlockSpec((MLA_HEADS * MLA_V, tm), lambda i: (0, i))] + tail_specs
        out_shape = [jax.ShapeDtypeStruct((n, hp), BF16), jax.ShapeDtypeStruct((n, hp), BF16),
                     jax.ShapeDtypeStruct((MLA_HEADS * MLA_V, n), BF16)] + tail_shapes
    return pl.pallas_call(
        functools.partial(_mla_prep_body, sample=sample),
        grid=(n // tm,),
        in_specs=in_specs, out_specs=out_specs, out_shape=out_shape,
        compiler_params=_cparams("parallel"),
        name="mla_prep_sample" if sample else "mla_prep_prompt",
    )(z, z, z, tab, nq.reshape(1, -1), nkv.reshape(1, -1), gq_ext, gk_ext, wuq_ext, wuk, wx)


def _flash_body(q_ref, k_ref, vt_ref, o_ref, m_sc, l_sc, acc_sc, *, heads, sub):
    i = pl.program_id(2)
    j = pl.program_id(3)

    @pl.when(j == 0)
    def _():
        m_sc[...] = jnp.full(m_sc.shape, NEG_INF, F32)
        l_sc[...] = jnp.zeros(l_sc.shape, F32)
        acc_sc[...] = jnp.zeros(acc_sc.shape, F32)

    tile = q_ref.shape[0]

    def step(diagonal):
        def q_cols(c, carry):
            r0 = pl.multiple_of(c * sub, sub)
            cols = pl.ds(r0, sub)
            vrows = [slice(h * MLA_V, (h + 1) * MLA_V) for h in range(heads)]
            prev = [(m_sc[h, :, cols], l_sc[h, :, cols], acc_sc[vrows[h], cols]) for h in range(heads)]
            new = []

            def scores(h):
                qk = slice(h * MLA_HEAD_PAD, (h + 1) * MLA_HEAD_PAD)
                return _dot_nt(k_ref[:, qk], q_ref[cols, qk])

            ahead = 3
            pending = [scores(h) for h in range(min(ahead, heads))]
            for h in range(heads):
                s = pending.pop(0)
                if diagonal:
                    keep = (lax.broadcasted_iota(jnp.int32, s.shape, 0)
                            <= r0 + lax.broadcasted_iota(jnp.int32, s.shape, 1))
                    s = jnp.where(keep, s, NEG_INF)
                m_prev, l_prev, acc_prev = prev[h]
                m_new = jnp.maximum(m_prev, jnp.max(s, 0, keepdims=True))
                alpha = jnp.exp2(m_prev - m_new)
                p = jnp.exp2(s - m_new)
                l_new = alpha * l_prev + jnp.sum(p, 0, keepdims=True)
                acc = alpha * acc_prev + _dot(vt_ref[vrows[h], :], p.astype(BF16))
                new.append((m_new, l_new, acc))
                if h + ahead < heads:
                    pending.append(scores(h + ahead))
            for h, (m_new, l_new, acc) in enumerate(new):
                if diagonal:
                    o_ref[vrows[h], cols] = acc / l_new
                else:
                    m_sc[h, :, cols] = m_new
                    l_sc[h, :, cols] = l_new
                    acc_sc[vrows[h], cols] = acc
            return carry

        lax.fori_loop(0, tile // sub, q_cols, 0)

    pl.when(j < i)(lambda: step(False))
    pl.when(j == i)(lambda: step(True))


def _flash(qt, kt, vt, tile, heads):
    b, s, _ = qt.shape
    nt = s // tile
    return pl.pallas_call(
        functools.partial(_flash_body, heads=heads, sub=min(FLASH_SUB, tile)),
        grid=(b, MLA_HEADS // heads, nt, nt),
        in_specs=[pl.BlockSpec((None, tile, heads * MLA_HEAD_PAD), lambda bb, h, i, j: (bb, i, h)),
                  pl.BlockSpec((None, tile, heads * MLA_HEAD_PAD), lambda bb, h, i, j: (bb, jnp.minimum(j, i), h)),
                  pl.BlockSpec((heads * MLA_V, tile), lambda bb, h, i, j: (h, bb * nt + jnp.minimum(j, i)))],
        out_specs=pl.BlockSpec((None, heads * MLA_V, tile), lambda bb, h, i, j: (bb, h, i)),
        out_shape=jax.ShapeDtypeStruct((b, MLA_HEADS * MLA_V, s), F32),
        scratch_shapes=[pltpu.VMEM((heads, 1, tile), F32), pltpu.VMEM((heads, 1, tile), F32),
                        pltpu.VMEM((heads * MLA_V, tile), F32)],
        compiler_params=_cparams("parallel", "parallel", "parallel", "arbitrary"),
        name="mla_prompt_attn",
    )(qt, kt, vt)


def _decode_body(pt_ref, ql_ref, qr_ref, cn_ref, krn_ref, irn_ref, wuv_ref, lat_pool, krt_pool, irt_pool,
                 o_ref, m_sc, l_sc, acc_sc, lat_buf, krt_buf, irt_buf, sems, *, pages, n_chunks, t_new):
    rows = MLA_HEADS * t_new
    page = lat_buf.shape[1] // pages
    t = pl.program_id(0)
    last = pl.num_programs(0) - 1
    c = lax.rem(t, n_chunks)
    slot = lax.rem(t, DECODE_SLOTS)

    def chunk_copies(step, buf_slot):
        copies = []
        for p in range(pages):
            pg = pt_ref[step * pages + p]
            keys = pl.ds(p * page, page)
            copies.append(pltpu.make_async_copy(lat_pool.at[pg], lat_buf.at[buf_slot, keys, :], sems.at[buf_slot, 0]))
            copies.append(pltpu.make_async_copy(krt_pool.at[pg], krt_buf.at[buf_slot, p], sems.at[buf_slot, 1]))
            copies.append(pltpu.make_async_copy(irt_pool.at[pg], irt_buf.at[buf_slot, p], sems.at[buf_slot, 2]))
        return copies

    @pl.when(t == 0)
    def _():
        for ahead in range(DECODE_SLOTS - 1):
            for cp in chunk_copies(jnp.minimum(ahead, last), ahead):
                cp.start()

    @pl.when(c == 0)
    def _():
        m_sc[...] = jnp.full(m_sc.shape, NEG_INF, F32)
        l_sc[...] = jnp.zeros(l_sc.shape, F32)
        acc_sc[...] = jnp.zeros(acc_sc.shape, F32)

    for cp in chunk_copies(t, slot):
        cp.wait()
    nxt_slot = lax.rem(t + DECODE_SLOTS - 1, DECODE_SLOTS)
    for cp in chunk_copies(jnp.minimum(t + DECODE_SLOTS - 1, last), nxt_slot):
        cp.start()

    ql = ql_ref[...].reshape(rows, MLA_KV_RANK).astype(BF16)
    qr = qr_ref[...].reshape(rows, MLA_ROPE).astype(BF16)

    def attend(parts, mask=None):
        ss = []
        for lat_b, krt_b, irt in parts:
            s = _dot_nt(ql, lat_b) + _dot(qr, krt_b)
            s = (s.reshape(MLA_HEADS, t_new, -1) * irt[:, None, :]).reshape(rows, -1)
            ss.append(s if mask is None else jnp.where(mask, s, NEG_INF))
        m_prev = m_sc[...]
        m_new = functools.reduce(jnp.maximum, [jnp.max(s, -1, keepdims=True) for s in ss], m_prev)
        alpha = jnp.exp(m_prev - m_new)
        l = alpha * l_sc[...]
        acc = alpha * acc_sc[...]
        for s, (lat_b, _, _) in zip(ss, parts):
            p = jnp.exp(s - m_new)
            l = l + jnp.sum(p, -1, keepdims=True)
            acc = acc + _dot(p.astype(BF16), lat_b)
        l_sc[...] = l
        acc_sc[...] = acc
        m_sc[...] = m_new

    n_parts = min(DECODE_PARTS, pages)
    pp = pages // n_parts
    attend([(lat_buf[slot, i * pp * page:(i + 1) * pp * page, :].astype(BF16),
             jnp.concatenate([krt_buf[slot, p] for p in range(i * pp, (i + 1) * pp)], 1).astype(BF16),
             jnp.concatenate([irt_buf[slot, p] for p in range(i * pp, (i + 1) * pp)], 1)) for i in range(n_parts)])

    @pl.when(c == n_chunks - 1)
    def _():
        pad = 128 - t_new
        lb = jnp.concatenate([cn_ref[...], jnp.zeros((pad, MLA_KV_RANK), F32)], 0).astype(BF16)
        kb = jnp.concatenate([krn_ref[...], jnp.zeros((pad, MLA_ROPE), F32)], 0).T.astype(BF16)
        ib = jnp.concatenate([irn_ref[...], jnp.zeros((pad, 128), F32)], 0).T[:MLA_HEADS]
        tq = lax.rem(lax.broadcasted_iota(jnp.int32, (rows, 128), 0), t_new)
        key = lax.broadcasted_iota(jnp.int32, (rows, 128), 1)
        attend([(lb, kb, ib)], mask=key <= tq)
        o_lat = acc_sc[...] / l_sc[...]
        for h in range(MLA_HEADS):
            o_ref[:, h * MLA_V:(h + 1) * MLA_V] = _dot(o_lat[h * t_new:(h + 1) * t_new].astype(BF16), wuv_ref[h])

    @pl.when(t == last)
    def _():
        for ahead in range(1, DECODE_SLOTS):
            for cp in chunk_copies(last, lax.rem(t + ahead, DECODE_SLOTS)):
                cp.wait()


def _decode(page_table, ql, qr, c_new, kr_new, ir_new, wuv_h, lat_pool, krt_pool, irt_pool, pages):
    bs, n_pages = page_table.shape
    t_new = c_new.shape[1]
    page = lat_pool.shape[1]
    n_chunks = n_pages // pages
    keys = pages * page
    rows = MLA_HEADS * t_new
    per_b = lambda shape: pl.BlockSpec((None,) + shape, lambda t, pt: (t // n_chunks,) + (0,) * len(shape))
    hbm = pl.BlockSpec(memory_space=pl.ANY)
    in_specs = [pl.BlockSpec((MLA_HEADS, None, t_new, MLA_KV_RANK), lambda t, pt: (0, t // n_chunks, 0, 0)),
                pl.BlockSpec((MLA_HEADS, None, t_new, MLA_ROPE), lambda t, pt: (0, t // n_chunks, 0, 0)),
                per_b((t_new, MLA_KV_RANK)), per_b((t_new, MLA_ROPE)), per_b((t_new, 128)),
                pl.BlockSpec(wuv_h.shape, lambda t, pt: (0, 0, 0)), hbm, hbm, hbm]
    grid_spec = pltpu.PrefetchScalarGridSpec(
        num_scalar_prefetch=1,
        grid=(bs * n_chunks,),
        in_specs=in_specs,
        out_specs=per_b((t_new, MLA_HEADS * MLA_V)),
        scratch_shapes=[pltpu.VMEM((rows, 1), F32), pltpu.VMEM((rows, 1), F32), pltpu.VMEM((rows, MLA_KV_RANK), F32),
                        pltpu.VMEM((DECODE_SLOTS, keys, MLA_KV_RANK), F32),
                        pltpu.VMEM((DECODE_SLOTS, pages, MLA_ROPE, page), F32),
                        pltpu.VMEM((DECODE_SLOTS, pages, MLA_HEADS, page), F32),
                        pltpu.SemaphoreType.DMA((DECODE_SLOTS, 3))],
    )
    return pl.pallas_call(
        functools.partial(_decode_body, pages=pages, n_chunks=n_chunks, t_new=t_new),
        grid_spec=grid_spec,
        out_shape=jax.ShapeDtypeStruct((bs, t_new, MLA_HEADS * MLA_V), F32),
        compiler_params=_cparams("arbitrary"),
        name="mla_sample_attn",
    )(page_table.reshape(-1), ql, qr, c_new, kr_new, ir_new, wuv_h, lat_pool, krt_pool, irt_pool)


def _gla_body(gq_ref, gk_ref, gv_ref, r_ref, alr_ref, wga_ref, bga_ref, gn_ref, *rest, chunk, sub, nb, carry):
    if carry:
        y_ref, s_out_ref = rest
        s_in_ref = s_out_ref

        @pl.when(pl.program_id(0) == 0)
        def _():
            s_out_ref[...] = jnp.zeros(s_out_ref.shape, F32)
    else:
        s_in_ref, y_ref, s_out_ref = rest
    row = lax.broadcasted_iota(jnp.int32, (chunk, chunk), 0)
    col = lax.broadcasted_iota(jnp.int32, (chunk, chunk), 1)
    tri = (col <= row).astype(F32)
    ones_cv = jnp.ones((chunk, GLA_DV), F32)
    brow = lax.broadcasted_iota(jnp.int32, (sub, chunk), 0)
    bcol = lax.broadcasted_iota(jnp.int32, (sub, chunk), 1)
    hi = lax.Precision.HIGHEST
    gn = gn_ref[...]

    def group(bis):
        chains = [(n, h) for n in range(len(bis)) for h in range(GLA_HEADS)]
        ks = lambda h: slice(h * GLA_DK, (h + 1) * GLA_DK)
        vs = lambda h: slice(h * GLA_DV, (h + 1) * GLA_DV)
        gate = []
        for bi in bis:
            a = _dot(alr_ref[bi].astype(BF16), wga_ref[...]) + bga_ref[...]
            gate.append((jnp.minimum(a, 0.0) - jnp.log1p(jnp.exp(-jnp.abs(a)))) / GLA_TAU)
        cum = [_dot(tri, g, precision=hi) for g in gate]
        q = {(n, h): gq_ref[bis[n], :, ks(h)] * GLA_DK ** -0.5 for n, h in chains}
        k = {(n, h): gk_ref[bis[n], :, ks(h)] for n, h in chains}
        vb = {(n, h): gv_ref[bis[n], :, vs(h)].astype(BF16) for n, h in chains}
        s0 = {(n, h): s_in_ref[bis[n], h] for n, h in chains}
        bc = {(n, h): cum[n][:, ks(h)] for n, h in chains}
        o = {ch: _dot((q[ch] * jnp.exp(bc[ch])).astype(BF16), s0[ch].astype(BF16)) for ch in chains}
        for n, h in chains:
            ch = (n, h)
            dec = jnp.exp(_dot_tn(gate[n][:, ks(h)], ones_cv, precision=hi))
            kd = (k[ch] * jnp.exp(bc[ch][chunk - 1:chunk] - bc[ch])).astype(BF16)
            s_out_ref[bis[n], h] = dec * s0[ch] + _dot_tn(kd, vb[ch])
        for blk in range(chunk // sub):
            r0 = blk * sub
            a_blk = {}
            for ch in chains:
                qb = q[ch][r0:r0 + sub]
                bb = bc[ch][r0:r0 + sub]
                ab = jnp.zeros((sub, chunk), F32)
                for s in range(sub):
                    e = jnp.exp(jnp.minimum(bb - bb[s:s + 1], 0.0))
                    cs = jnp.sum(qb * e * k[ch][r0 + s:r0 + s + 1], -1, keepdims=True)
                    ab = jnp.where(bcol == r0 + s, jnp.where(brow >= s, cs, 0.0), ab)
                if blk > 0:
                    b0 = bc[ch][r0 - 1:r0]
                    qs = (qb * jnp.exp(bb - b0)).astype(BF16)
                    ksc = (k[ch] * jnp.exp(jnp.minimum(b0 - bc[ch], 0.0))).astype(BF16)
                    ab = jnp.where(bcol < r0, _dot_nt(qs, ksc), ab)
                a_blk[ch] = ab
            for n, h in chains:
                ch = (n, h)
                ob = o[ch][r0:r0 + sub] + _dot(a_blk[ch].astype(BF16), vb[ch])
                rr = r_ref[bis[n], r0:r0 + sub, vs(h)]
                y_ref[bis[n], r0:r0 + sub, vs(h)] = _rms(ob, gn) * (rr * jax.nn.sigmoid(rr))

    if carry:
        group(list(range(nb)))
    else:
        def rows(i, carry_):
            group([i * GLA_ROWS_PER_ITER + j for j in range(GLA_ROWS_PER_ITER)])
            return carry_

        lax.fori_loop(0, nb // GLA_ROWS_PER_ITER, rows, 0)


def _gla(z3, state, wga, bga, gn, chunk, sub, nb):
    b, s, _ = z3.shape
    carry = state is None
    if carry:
        grid = (s // chunk,)
        zmap = lambda blk: (lambda c: (0, c, blk))
        smap = lambda c: (0, 0, 0, 0)
        nb = b
    else:
        grid = (b // nb,)
        zmap = lambda blk: (lambda i: (i, 0, blk))
        smap = lambda i: (i, 0, 0, 0)
    zspec = lambda wb: pl.BlockSpec((nb, chunk, wb[0]), zmap(wb[1]))
    const = lambda shape: pl.BlockSpec(shape, lambda i: (0,) * len(shape))
    sspec = pl.BlockSpec((nb, GLA_HEADS, GLA_DK, GLA_DV), smap)
    in_specs = [zspec(ZB_GQ), zspec(ZB_GK), zspec(ZB_GV), zspec(ZB_R), zspec(ZB_ALR),
                const(wga.shape), const(bga.shape), const(gn.shape)]
    args = [z3, z3, z3, z3, z3, wga, bga, gn]
    if not carry:
        in_specs.append(sspec)
        args.append(state)
    return pl.pallas_call(
        functools.partial(_gla_body, chunk=chunk, sub=sub, nb=nb, carry=carry),
        grid=grid,
        in_specs=in_specs,
        out_specs=[pl.BlockSpec((nb, chunk, GLA_HEADS * GLA_DV), zmap(0)), sspec],
        out_shape=[jax.ShapeDtypeStruct((b, s, GLA_HEADS * GLA_DV), F32),
                   jax.ShapeDtypeStruct((b, GLA_HEADS, GLA_DK, GLA_DV), F32)],
        compiler_params=_cparams("arbitrary"),
        name="gla_prompt" if carry else "gla_sample",
    )(*args)


def _out_proj_body(mla_ref, gla_ref, x_ref, nm_ref, w_ref, o_ref, *, mla_feature_major):
    mla = mla_ref[...].T if mla_feature_major else mla_ref[...]
    half = mla.shape[1]
    mn = _rms(mla, nm_ref[...]).astype(BF16)
    o_ref[...] = x_ref[...] + _dot(mn, w_ref[:half]) + _dot(gla_ref[...].astype(BF16), w_ref[half:])


def _out_proj(mla_o, gla_y, x, nm, w, tm):
    n, d = x.shape
    half = gla_y.shape[1]
    feature_major = mla_o.ndim == 3
    if feature_major:
        per_b = mla_o.shape[2] // tm
        mla_spec = pl.BlockSpec((None, half, tm), lambda i: (i // per_b, 0, lax.rem(i, per_b)))
    else:
        mla_spec = pl.BlockSpec((tm, half), lambda i: (i, 0))
    return pl.pallas_call(
        functools.partial(_out_proj_body, mla_feature_major=feature_major),
        grid=(n // tm,),
        in_specs=[mla_spec, pl.BlockSpec((tm, half), lambda i: (i, 0)),
                  pl.BlockSpec((tm, d), lambda i: (i, 0)), pl.BlockSpec((1, half), lambda i: (0, 0)),
                  pl.BlockSpec(w.shape, lambda i: (0, 0))],
        out_specs=pl.BlockSpec((tm, d), lambda i: (i, 0)),
        out_shape=jax.ShapeDtypeStruct((n, d), F32),
        compiler_params=_cparams("parallel"),
        name="out_proj",
    )(mla_o, gla_y, x, nm.reshape(1, half), w)


def _top_desc(s, n):
    vals = []
    w = s
    for _ in range(n):
        m = jnp.max(w, axis=0, keepdims=True)
        vals.append(m)
        w = jnp.where(w == m, NEG_INF, w)
    return vals


def _peer_route_body(h_ref, g_ref, wqt_ref, k1_ref, k2_ref, xnt_ref, s2_ref, e2_ref, thr_ref, e1z_ref):
    xnt = _rms(h_ref[...], g_ref[...]).T.astype(BF16)
    xnt_ref[...] = xnt
    qt = _dot(wqt_ref[...], xnt).astype(BF16)
    half = PEER_N_KEYS
    for h in range(PEER_HEADS):
        s1 = _dot(k1_ref[...], qt[(2 * h) * half:(2 * h + 1) * half])
        s2 = _dot(k2_ref[...], qt[(2 * h + 1) * half:(2 * h + 2) * half])
        v1 = _top_desc(s1, PEER_TOPK)
        v2 = _top_desc(s2, PEER_TOPK)
        pairs = [(a, b) for a in range(PEER_TOPK) for b in range(PEER_TOPK // (a + 1))]
        sums = {ab: v1[ab[0]] + v2[ab[1]] for ab in pairs}
        npad = -len(pairs) % 8
        stack = jnp.concatenate([sums[ab] for ab in pairs] + [jnp.full_like(v1[0], NEG_INF)] * npad, 0)
        tau = _top_desc(stack, PEER_TOPK)[-1]
        top = v1[0] + v2[0]
        zsum = jnp.sum(jnp.where(stack >= tau, jnp.exp(stack - top), 0.0), axis=0, keepdims=True)
        thr = jnp.full(s1.shape, POS_INF, F32)
        for a in range(PEER_TOPK):
            ta = jnp.full_like(tau, POS_INF)
            for b in range(PEER_TOPK // (a + 1)):
                ta = jnp.minimum(ta, jnp.where(sums[(a, b)] >= tau, v2[b], POS_INF))
            thr = jnp.where(s1 == v1[a], ta, thr)
        e2 = jnp.exp(s2 - v2[0])
        e1z = jnp.exp(s1 - v1[0]) / zsum
        for lg in range(thr.shape[1] // 128):
            lanes = slice(lg * 128, (lg + 1) * 128)
            s2_ref[h, lg] = s2[:, lanes]
            e2_ref[h, lg] = e2[:, lanes]
            thr_ref[h, lg] = thr[:, lanes]
            e1z_ref[h, lg] = e1z[:, lanes]


def _peer_route(h, g, wqt, k1, k2, tt):
    n, d = h.shape
    per_group = jax.ShapeDtypeStruct((PEER_HEADS, n // 128, PEER_N_KEYS, 128), F32)
    gspec = pl.BlockSpec((PEER_HEADS, tt // 128, PEER_N_KEYS, 128), lambda i: (0, i, 0, 0))
    return pl.pallas_call(
        _peer_route_body,
        grid=(n // tt,),
        in_specs=[pl.BlockSpec((tt, d), lambda i: (i, 0)), pl.BlockSpec((1, d), lambda i: (0, 0)),
                  pl.BlockSpec(wqt.shape, lambda i: (0, 0)), pl.BlockSpec(k1.shape, lambda i: (0, 0)),
                  pl.BlockSpec(k2.shape, lambda i: (0, 0))],
        out_specs=[pl.BlockSpec((d, tt), lambda i: (0, i)), gspec, gspec, gspec, gspec],
        out_shape=[jax.ShapeDtypeStruct((d, n), BF16), per_group, per_group, per_group, per_group],
        compiler_params=_cparams("parallel"),
        name="peer_route",
    )(h, g.reshape(1, d), wqt, k1, k2)


def _peer_dense_body(xnt_ref, u_ref, vt_ref, s2_ref, e2_ref, thr_ref, e1z_ref, yt_ref, act_sc, ca_sc):
    @pl.when(pl.program_id(1) == 0)
    def _():
        yt_ref[...] = jnp.zeros(yt_ref.shape, F32)

    n_lg, te, _ = act_sc.shape
    act = _dot(u_ref[...], xnt_ref[...])
    act = 0.5 * act * (1.0 + lax.erf(act * (2.0 ** -0.5)))
    for lg in range(n_lg):
        act_sc[lg] = act[:, lg * 128:(lg + 1) * 128]
    groups = te // PEER_N_KEYS
    n_sub = PEER_N_KEYS // PEER_SUB

    def gate_block(blk, carry):
        lg = blk // n_sub
        k0 = pl.multiple_of(lax.rem(blk, n_sub) * PEER_SUB, PEER_SUB)
        coef = [jnp.zeros((PEER_SUB, 128), F32)] * groups
        for h in range(PEER_HEADS):
            s2b = s2_ref[h, lg, pl.ds(k0, PEER_SUB), :]
            e2b = e2_ref[h, lg, pl.ds(k0, PEER_SUB), :]
            for gi in range(groups):
                thr = jnp.broadcast_to(thr_ref[h, lg, gi:gi + 1, :], (PEER_SUB, 128))
                e1z = jnp.broadcast_to(e1z_ref[h, lg, gi:gi + 1, :], (PEER_SUB, 128))
                coef[gi] = coef[gi] + jnp.where(s2b >= thr, e2b, 0.0) * e1z
        for gi in range(groups):
            rows = pl.ds(pl.multiple_of(gi * PEER_N_KEYS + k0, PEER_SUB), PEER_SUB)
            ca_sc[lg, rows, :] = (coef[gi] * act_sc[lg, rows, :]).astype(BF16)
        return carry

    lax.fori_loop(0, n_lg * n_sub, gate_block, 0)
    ca = jnp.concatenate([ca_sc[lg] for lg in range(n_lg)], 1)
    yt_ref[...] += _dot(vt_ref[...], ca)


def _peer_dense(xnt, u, vt, s2, e2, thr, e1z, tt, te):
    d, n = xnt.shape
    ne = u.shape[0] // te
    gpt = te // PEER_N_KEYS
    full = pl.BlockSpec((PEER_HEADS, tt // 128, PEER_N_KEYS, 128), lambda i, j: (0, i, 0, 0))
    part = pl.BlockSpec((PEER_HEADS, tt // 128, gpt, 128), lambda i, j: (0, i, j, 0))
    return pl.pallas_call(
        _peer_dense_body,
        grid=(n // tt, ne),
        in_specs=[pl.BlockSpec((d, tt), lambda i, j: (0, i)), pl.BlockSpec((te, d), lambda i, j: (j, 0)),
                  pl.BlockSpec((d, te), lambda i, j: (0, j)), full, full, part, part],
        out_specs=pl.BlockSpec((d, tt), lambda i, j: (0, i)),
        out_shape=jax.ShapeDtypeStruct((d, n), F32),
        scratch_shapes=[pltpu.VMEM((tt // 128, te, 128), F32), pltpu.VMEM((tt // 128, te, 128), BF16)],
        compiler_params=_cparams("parallel", "arbitrary"),
        name="peer_dense",
    )(xnt, u, vt, s2, e2, thr, e1z)


def _ple_body(h_ref, yt_ref, p_ref, g_ref, wg_ref, wp_ref, o_ref):
    h2 = h_ref[...] + yt_ref[...].T
    gate = jax.nn.sigmoid(_dot(_rms(h2, g_ref[...]).astype(BF16), wg_ref[...]))
    o_ref[...] = h2 + gate * _dot(p_ref[...].astype(BF16), wp_ref[...])


def _ple(h, yt, p, g, wg, wp, tm):
    n, d = h.shape
    return pl.pallas_call(
        _ple_body,
        grid=(n // tm,),
        in_specs=[pl.BlockSpec((tm, d), lambda i: (i, 0)), pl.BlockSpec((d, tm), lambda i: (0, i)),
                  pl.BlockSpec((tm, p.shape[1]), lambda i: (i, 0)), pl.BlockSpec((1, d), lambda i: (0, 0)),
                  pl.BlockSpec(wg.shape, lambda i: (0, 0)), pl.BlockSpec(wp.shape, lambda i: (0, 0))],
        out_specs=pl.BlockSpec((tm, d), lambda i: (i, 0)),
        out_shape=jax.ShapeDtypeStruct((n, d), F32),
        compiler_params=_cparams("parallel"),
        name="ple",
    )(h, yt, p, g.reshape(1, d), wg, wp)


def _rope_table(pos):
    half = MLA_ROPE // 2
    inv_freq = ROPE_THETA ** (-jnp.arange(half, dtype=F32) / half)
    ang = pos.astype(F32)[:, None] * inv_freq
    cos, sin = jnp.cos(ang), jnp.sin(ang)
    return jnp.concatenate([cos, cos, sin, sin], -1)


def _tile(n, t):
    return t if n % t == 0 else n


def _finish(x2, mla_o, gla_y, p2, w):
    n = x2.shape[0]
    h = _out_proj(mla_o, gla_y, x2, w["norm_mla_out"], w["w_out"], _tile(n, TM_OUT))
    xnt, s2, e2, thr, e1z = _peer_route(h, w["norm_ffn"], w["wqt"], w["k1"], w["k2"], _tile(n, TT_ROUTE))
    yt = _peer_dense(xnt, w["u"], w["vt"], s2, e2, thr, e1z, _tile(n, TT_DENSE), TE_DENSE)
    return _ple(h, yt, p2, w["norm_ple"], w["w_ple_gate"], w["w_ple_proj"], _tile(n, TM_PLE))


def kernel(x_prompt, x_sample, cache_kv_latent, cache_k_rope, cache_k_inv_rms, state_gla, page_table, p_prompt,
           p_sample, norm_mix, w_in, norm_q_lat, w_uq, norm_kv_lat, w_uk, w_uv, qk_gain_q, qk_gain_k, norm_mla_out,
           w_gla_a, b_gla_a, norm_gla_out, w_out, norm_ffn, peer_w_q, peer_keys1, peer_keys2, peer_u, peer_v,
           norm_ple, w_ple_gate, w_ple_proj):
    depth = w_in.shape[0]
    b, s, d = x_prompt.shape
    bs, t_new, _ = x_sample.shape
    n_pages = page_table.shape[1]
    page = cache_kv_latent.shape[2]
    past_len = n_pages * page
    tab_p = _rope_table(jnp.arange(s))
    tab_s = _rope_table(past_len + jnp.arange(t_new))
    hp, hs = x_prompt.reshape(b * s, d), x_sample.reshape(bs * t_new, d)
    outs = [[] for _ in range(8)]
    for i in range(depth):
        wi = w_in[i]
        cuts = [0, 512, 768, 832, 1344, 1856, 2880, 3904, 3920]
        cq_w, ckv_w, kr_w, gq_w, gk_w, gv_w, r_w, alr_w = [wi[:, cuts[j]:cuts[j + 1]] for j in range(8)]
        w_in_ext = jnp.concatenate(
            [cq_w, gq_w, gk_w, ckv_w, kr_w, _rot_cols(kr_w), alr_w,
             jnp.zeros((d, 128 - GLA_GATE_RANK), F32), gv_w, r_w], 1).astype(BF16)
        wuq3 = w_uq[i].reshape(MLA_Q_RANK, MLA_HEADS, MLA_QK)
        wuq_ext = jnp.concatenate([wuq3, _rot_cols(wuq3[..., MLA_NOPE:])], -1).reshape(MLA_Q_RANK, -1).astype(BF16)
        gq, gk = qk_gain_q[i], qk_gain_k[i]
        gq_ext = jnp.concatenate([gq, _rot_gain(gq[MLA_NOPE:])]).reshape(1, -1)
        gk_ext = jnp.concatenate([gk, _rot_gain(gk[MLA_NOPE:])]).reshape(1, -1)
        wuk = w_uk[i].reshape(MLA_KV_RANK, -1).astype(BF16)
        wuv = w_uv[i].reshape(MLA_KV_RANK, -1).astype(BF16)
        wuk_t = jnp.transpose(w_uk[i], (1, 2, 0)).astype(BF16)
        wuv_h = jnp.transpose(w_uv[i], (1, 0, 2)).astype(BF16)
        wga = jnp.concatenate([w_gla_a[i], jnp.zeros((128 - GLA_GATE_RANK, w_gla_a.shape[2]), F32)], 0).astype(BF16)
        bga = b_gla_a[i].reshape(1, -1)
        gn = norm_gla_out[i].reshape(1, -1)
        w = {"norm_mla_out": norm_mla_out[i], "w_out": w_out[i].astype(BF16), "norm_ffn": norm_ffn[i],
             "wqt": peer_w_q[i].T.astype(BF16), "k1": peer_keys1[i].astype(BF16), "k2": peer_keys2[i].astype(BF16),
             "u": peer_u[i].astype(BF16), "vt": peer_v[i].T.astype(BF16), "norm_ple": norm_ple[i],
             "w_ple_gate": w_ple_gate[i].astype(BF16), "w_ple_proj": w_ple_proj[i].astype(BF16)}
        prep = (norm_q_lat[i], norm_kv_lat[i], gq_ext, gk_ext, wuq_ext, wuk)

        n = b * s
        z = _norm_matmul(hp, norm_mix[i], w_in_ext, _tile(n, TM_IN), TN_IN)
        tm = _tile(s, TM_PREP)
        qt, kt, vt, lat, kr, ir = _mla_prep(z, tab_p, s // tm, *prep, wuv, tm, sample=False)
        tq = _tile(s, FLASH_TILE)
        mla_o = _flash(qt.reshape(b, s, -1), kt.reshape(b, s, -1), vt, tq, FLASH_HEADS)
        gla_y, s_fin = _gla(z.reshape(b, s, Z_COLS), None, wga, bga, gn, GLA_CHUNK, GLA_SUB, b)
        hp_next = _finish(hp, mla_o, gla_y.reshape(n, -1), p_prompt[i].reshape(n, -1), w)
        outs[0].append(lat.reshape(b, s, -1))
        outs[1].append(kr.reshape(b, s, -1))
        outs[2].append(ir[:, :MLA_HEADS].reshape(b, s, -1))
        outs[3].append(s_fin)

        n = bs * t_new
        z = _norm_matmul(hs, norm_mix[i], w_in_ext, _tile(n, TM_IN), TN_IN)
        tm = _tile(n, TM_PREP)
        tab = jnp.tile(tab_s, (tm // t_new, 1))
        ql, qr, lat, kr, ir = _mla_prep(z, tab, 1, *prep, wuk_t, tm, sample=True)
        mla_o = _decode(page_table, ql.reshape(MLA_HEADS, bs, t_new, -1), qr.reshape(MLA_HEADS, bs, t_new, -1),
                        lat.reshape(bs, t_new, -1), kr.reshape(bs, t_new, -1), ir.reshape(bs, t_new, -1), wuv_h,
                        cache_kv_latent[i], jnp.transpose(cache_k_rope[i], (0, 2, 1)),
                        jnp.transpose(cache_k_inv_rms[i], (0, 2, 1)), _tile(n_pages, DECODE_PAGES))
        gla_y, s_new = _gla(z.reshape(bs, t_new, Z_COLS), state_gla[i], wga, bga, gn, t_new, t_new,
                            _tile(bs, GLA_SAMPLE_ROWS))
        hs_next = _finish(hs, mla_o.reshape(n, -1), gla_y.reshape(n, -1), p_sample[i].reshape(n, -1), w)
        outs[4].append(lat.reshape(bs, t_new, -1))
        outs[5].append(kr.reshape(bs, t_new, -1))
        outs[6].append(ir[:, :MLA_HEADS].reshape(bs, t_new, -1))
        outs[7].append(s_new)
        hp, hs = hp_next, hs_next
    return (hp.reshape(b, s, d), hs.reshape(bs, t_new, d)) + tuple(jnp.stack(o, 0) for o in outs)
```

```python
import functools

import jax
import jax.numpy as jnp
from jax import lax
from jax.experimental import pallas as pl
from jax.experimental.pallas import tpu as pltpu

F32 = jnp.float32
BF16 = jnp.bfloat16
EPS = 1e-6
NEG_INF = float("-inf")
POS_INF = float("inf")

MLA_HEADS = 8
MLA_NOPE = 128
MLA_ROPE = 64
MLA_QK = MLA_NOPE + MLA_ROPE
MLA_V = 128
MLA_Q_RANK = 512
MLA_KV_RANK = 256
MLA_SCALE = MLA_QK ** -0.5
LOG2E = 1.4426950408889634
MLA_HEAD_PAD = 256
ROPE_THETA = 10000.0
GLA_HEADS = 4
GLA_DK = 128
GLA_DV = 256
GLA_GATE_RANK = 16
GLA_TAU = 16.0
GLA_CHUNK = 64
GLA_SUB = 16
PEER_HEADS = 8
PEER_N_KEYS = 128
PEER_TOPK = 16
PEER_SUB = 32
PLE_DIM = 256

Z_COLS = 4096
ZB_CQ = (512, 0)
ZB_GQ = (512, 1)
ZB_GK = (512, 2)
ZB_CKV = (256, 6)
ZB_KR = (128, 14)
ZB_ALR = (128, 15)
ZB_GV = (1024, 2)
ZB_R = (1024, 3)

V7X_VMEM_LIMIT = 56 * 1024 * 1024

TM_IN, TN_IN = 1024, 1024
TM_PREP = 256
FLASH_TILE = 512
FLASH_HEADS = 8
FLASH_SUB = 512
DECODE_PAGES = 64
DECODE_PARTS = 8
DECODE_SLOTS = 3
GLA_SAMPLE_ROWS = 8
GLA_ROWS_PER_ITER = 2
TM_OUT = 256
TT_ROUTE = 256
TT_DENSE, TE_DENSE = 512, 1024
TM_PLE = 256


def _cparams(*sem):
    return pltpu.CompilerParams(dimension_semantics=sem, vmem_limit_bytes=V7X_VMEM_LIMIT)


def _dot(a, b, **kw):
    return jnp.dot(a, b, preferred_element_type=F32, **kw)


def _dot_nt(a, b):
    return lax.dot_general(a, b, (((1,), (1,)), ((), ())), preferred_element_type=F32)


def _dot_tn(a, b, **kw):
    return lax.dot_general(a, b, (((0,), (0,)), ((), ())), preferred_element_type=F32, **kw)


def _rms(x, g):
    return x * lax.rsqrt(jnp.mean(x * x, -1, keepdims=True) + EPS) * g


def _rot_cols(w):
    h = w.shape[-1] // 2
    return jnp.concatenate([-w[..., h:], w[..., :h]], -1)


def _rot_gain(g):
    h = g.shape[-1] // 2
    return jnp.concatenate([g[..., h:], g[..., :h]], -1)


def _norm_matmul_body(x_ref, g_ref, w_ref, o_ref, xn_sc):
    @pl.when(pl.program_id(1) == 0)
    def _():
        xn_sc[...] = _rms(x_ref[...], g_ref[...]).astype(BF16)

    o_ref[...] = _dot(xn_sc[...], w_ref[...])


def _norm_matmul(x, g, w, tm, tn):
    n, d = x.shape
    nc = w.shape[1]
    return pl.pallas_call(
        _norm_matmul_body,
        grid=(n // tm, nc // tn),
        in_specs=[pl.BlockSpec((tm, d), lambda i, j: (i, 0)),
                  pl.BlockSpec((1, d), lambda i, j: (0, 0)),
                  pl.BlockSpec((d, tn), lambda i, j: (0, j))],
        out_specs=pl.BlockSpec((tm, tn), lambda i, j: (i, j)),
        out_shape=jax.ShapeDtypeStruct((n, nc), F32),
        scratch_shapes=[pltpu.VMEM((tm, d), BF16)],
        compiler_params=_cparams("parallel", "arbitrary"),
        name="in_proj",
    )(x, g.reshape(1, d), w)


def _mla_prep_body(cq_ref, ckv_ref, krb_ref, tab_ref, nq_ref, nkv_ref, gq_ref, gk_ref, wuq_ref, wuk_ref, wx_ref,
                   *outs, sample):
    if sample:
        ql_ref, qr_ref, lat_ref, kr_ref, ir_ref = outs
    else:
        qt_ref, kt_ref, vt_ref, lat_ref, kr_ref, ir_ref = outs
    tab = tab_ref[...]
    gq = gq_ref[...]
    gk = gk_ref[...]
    cqn = _rms(cq_ref[...], nq_ref[...]).astype(BF16)
    qraw = _dot(cqn, wuq_ref[...])
    c = _rms(ckv_ref[...], nkv_ref[...])
    lat_ref[...] = c
    cb = c.astype(BF16)
    kn = _dot(cb, wuk_ref[...])
    if not sample:
        vt_ref[...] = _dot(cb, wx_ref[...]).T.astype(BF16)
    krb = krb_ref[...]
    lane = lax.broadcasted_iota(jnp.int32, krb.shape, 1)
    ss_kr = jnp.sum(jnp.where(lane < MLA_ROPE, krb * krb, 0.0), -1, keepdims=True)
    t = krb * gk[:, MLA_NOPE:] * tab
    kr2 = t + pltpu.roll(t, MLA_ROPE, 1)
    kr_ref[...] = kr2[:, :MLA_ROPE]
    krz = jnp.where(lane < MLA_ROPE, kr2, 0.0)
    lane_q = lax.broadcasted_iota(jnp.int32, (krb.shape[0], MLA_HEAD_PAD), 1)
    ir_acc = jnp.zeros(krb.shape, F32)
    for h in range(MLA_HEADS):
        knh = kn[:, h * MLA_NOPE:(h + 1) * MLA_NOPE]
        ss = jnp.sum(knh * knh, -1, keepdims=True) + ss_kr
        inv = lax.rsqrt(ss / MLA_QK + EPS)
        ir_acc = jnp.where(lane == h, inv, ir_acc)
        qh = qraw[:, h * MLA_HEAD_PAD:(h + 1) * MLA_HEAD_PAD]
        ssq = jnp.sum(jnp.where(lane_q < MLA_QK, qh * qh, 0.0), -1, keepdims=True)
        qinv = lax.rsqrt(ssq / MLA_QK + EPS)
        qn = qh[:, :MLA_NOPE] * qinv * gq[:, :MLA_NOPE]
        tq = qh[:, MLA_NOPE:] * qinv * gq[:, MLA_NOPE:] * tab
        qr2 = tq + pltpu.roll(tq, MLA_ROPE, 1)
        if sample:
            qg = (qn * gk[:, :MLA_NOPE]).astype(BF16)
            ql_ref[h] = _dot(qg, wx_ref[h]) * MLA_SCALE
            qr_ref[h] = qr2[:, :MLA_ROPE] * MLA_SCALE
        else:
            lo = h * MLA_HEAD_PAD
            qt_ref[:, lo:lo + MLA_NOPE] = (qn * (MLA_SCALE * LOG2E)).astype(BF16)
            qt_ref[:, lo + MLA_NOPE:lo + MLA_HEAD_PAD] = (qr2 * (MLA_SCALE * LOG2E)).astype(BF16)
            kt_ref[:, lo:lo + MLA_NOPE] = (knh * gk[:, :MLA_NOPE] * inv).astype(BF16)
            kt_ref[:, lo + MLA_NOPE:lo + MLA_HEAD_PAD] = (krz * inv).astype(BF16)
    ir_ref[...] = ir_acc


def _mla_prep(z, tab, tab_blocks, nq, nkv, gq_ext, gk_ext, wuq_ext, wuk, wx, tm, sample):
    n = z.shape[0]
    hp = MLA_HEADS * MLA_HEAD_PAD

    def const(shape):
        return pl.BlockSpec(shape, lambda i: (0,) * len(shape))

    in_specs = [pl.BlockSpec((tm, ZB_CQ[0]), lambda i: (i, ZB_CQ[1])),
                pl.BlockSpec((tm, ZB_CKV[0]), lambda i: (i, ZB_CKV[1])),
                pl.BlockSpec((tm, ZB_KR[0]), lambda i: (i, ZB_KR[1])),
                pl.BlockSpec((tm, 128), lambda i: (i % tab_blocks, 0)),
                const((1, MLA_Q_RANK)), const((1, MLA_KV_RANK)), const((1, MLA_HEAD_PAD)), const((1, MLA_HEAD_PAD)),
                const(wuq_ext.shape), const(wuk.shape), const(wx.shape)]
    row = lambda w: pl.BlockSpec((tm, w), lambda i: (i, 0))
    tail_specs = [row(MLA_KV_RANK), row(MLA_ROPE), row(128)]
    tail_shapes = [jax.ShapeDtypeStruct((n, MLA_KV_RANK), F32), jax.ShapeDtypeStruct((n, MLA_ROPE), F32),
                   jax.ShapeDtypeStruct((n, 128), F32)]
    if sample:
        head = lambda w: pl.BlockSpec((MLA_HEADS, tm, w), lambda i: (0, i, 0))
        out_specs = [head(MLA_KV_RANK), head(MLA_ROPE)] + tail_specs
        out_shape = [jax.ShapeDtypeStruct((MLA_HEADS, n, MLA_KV_RANK), F32),
                     jax.ShapeDtypeStruct((MLA_HEADS, n, MLA_ROPE), F32)] + tail_shapes
    else:
        out_specs = [row(hp), row(hp), pl.BlockSpec((MLA_HEADS * MLA_V, tm), lambda i: (0, i))] + tail_specs
        out_shape = [jax.ShapeDtypeStruct((n, hp), BF16), jax.ShapeDtypeStruct((n, hp), BF16),
                     jax.ShapeDtypeStruct((MLA_HEADS * MLA_V, n), BF16)] + tail_shapes
    return pl.pallas_call(
        functools.partial(_mla_prep_body, sample=sample),
        grid=(n // tm,),
        in_specs=in_specs, out_specs=out_specs, out_shape=out_shape,
        compiler_params=_cparams("parallel"),
        name="mla_prep_sample" if sample else "mla_prep_prompt",
    )(z, z, z, tab, nq.reshape(1, -1), nkv.reshape(1, -1), gq_ext, gk_ext, wuq_ext, wuk, wx)


def _flash_body(q_ref, k_ref, vt_ref, o_ref, m_sc, l_sc, acc_sc, *, heads, sub):
    i = pl.program_id(2)
    j = pl.program_id(3)

    @pl.when(j == 0)
    def _():
        m_sc[...] = jnp.full(m_sc.shape, NEG_INF, F32)
        l_sc[...] = jnp.zeros(l_sc.shape, F32)
        acc_sc[...] = jnp.zeros(acc_sc.shape, F32)

    tile = q_ref.shape[0]

    def step(diagonal):
        def q_cols(c, carry):
            r0 = pl.multiple_of(c * sub, sub)
            cols = pl.ds(r0, sub)
            vrows = [slice(h * MLA_V, (h + 1) * MLA_V) for h in range(heads)]
            prev = [(m_sc[h, :, cols], l_sc[h, :, cols], acc_sc[vrows[h], cols]) for h in range(heads)]
            new = []

            def scores(h):
                qk = slice(h * MLA_HEAD_PAD, (h + 1) * MLA_HEAD_PAD)
                return _dot_nt(k_ref[:, qk], q_ref[cols, qk])

            ahead = 3
            pending = [scores(h) for h in range(min(ahead, heads))]
            for h in range(heads):
                s = pending.pop(0)
                if diagonal:
                    keep = (lax.broadcasted_iota(jnp.int32, s.shape, 0)
                            <= r0 + lax.broadcasted_iota(jnp.int32, s.shape, 1))
                    s = jnp.where(keep, s, NEG_INF)
                m_prev, l_prev, acc_prev = prev[h]
                m_new = jnp.maximum(m_prev, jnp.max(s, 0, keepdims=True))
                alpha = jnp.exp2(m_prev - m_new)
                p = jnp.exp2(s - m_new)
                l_new = alpha * l_prev + jnp.sum(p, 0, keepdims=True)
                acc = alpha * acc_prev + _dot(vt_ref[vrows[h], :], p.astype(BF16))
                new.append((m_new, l_new, acc))
                if h + ahead < heads:
                    pending.append(scores(h + ahead))
            for h, (m_new, l_new, acc) in enumerate(new):
                if diagonal:
                    o_ref[vrows[h], cols] = acc / l_new
                else:
                    m_sc[h, :, cols] = m_new
                    l_sc[h, :, cols] = l_new
                    acc_sc[vrows[h], cols] = acc
            return carry

        lax.fori_loop(0, tile // sub, q_cols, 0)

    pl.when(j < i)(lambda: step(False))
    pl.when(j == i)(lambda: step(True))


def _flash(qt, kt, vt, tile, heads):
    b, s, _ = qt.shape
    nt = s // tile
    return pl.pallas_call(
        functools.partial(_flash_body, heads=heads, sub=min(FLASH_SUB, tile)),
        grid=(b, MLA_HEADS // heads, nt, nt),
        in_specs=[pl.BlockSpec((None, tile, heads * MLA_HEAD_PAD), lambda bb, h, i, j: (bb, i, h)),
                  pl.BlockSpec((None, tile, heads * MLA_HEAD_PAD), lambda bb, h, i, j: (bb, jnp.minimum(j, i), h)),
                  pl.BlockSpec((heads * MLA_V, tile), lambda bb, h, i, j: (h, bb * nt + jnp.minimum(j, i)))],
        out_specs=pl.BlockSpec((None, heads * MLA_V, tile), lambda bb, h, i, j: (bb, h, i)),
        out_shape=jax.ShapeDtypeStruct((b, MLA_HEADS * MLA_V, s), F32),
        scratch_shapes=[pltpu.VMEM((heads, 1, tile), F32), pltpu.VMEM((heads, 1, tile), F32),
                        pltpu.VMEM((heads * MLA_V, tile), F32)],
        compiler_params=_cparams("parallel", "parallel", "parallel", "arbitrary"),
        name="mla_prompt_attn",
    )(qt, kt, vt)


def _decode_body(pt_ref, ql_ref, qr_ref, cn_ref, krn_ref, irn_ref, wuv_ref, lat_pool, krt_pool, irt_pool,
                 o_ref, m_sc, l_sc, acc_sc, lat_buf, krt_buf, irt_buf, sems, *, pages, n_chunks, t_new):
    rows = MLA_HEADS * t_new
    page = lat_buf.shape[1] // pages
    t = pl.program_id(0)
    last = pl.num_programs(0) - 1
    c = lax.rem(t, n_chunks)
    slot = lax.rem(t, DECODE_SLOTS)

    def chunk_copies(step, buf_slot):
        copies = []
        for p in range(pages):
            pg = pt_ref[step * pages + p]
            keys = pl.ds(p * page, page)
            copies.append(pltpu.make_async_copy(lat_pool.at[pg], lat_buf.at[buf_slot, keys, :], sems.at[buf_slot, 0]))
            copies.append(pltpu.make_async_copy(krt_pool.at[pg], krt_buf.at[buf_slot, p], sems.at[buf_slot, 1]))
            copies.append(pltpu.make_async_copy(irt_pool.at[pg], irt_buf.at[buf_slot, p], sems.at[buf_slot, 2]))
        return copies

    def start_all(copies):
        for n, cp in enumerate(copies):
            cp.start(priority=(n // 3) % 2)

    @pl.when(t == 0)
    def _():
        for ahead in range(DECODE_SLOTS - 1):
            start_all(chunk_copies(jnp.minimum(ahead, last), ahead))

    @pl.when(c == 0)
    def _():
        m_sc[...] = jnp.full(m_sc.shape, NEG_INF, F32)
        l_sc[...] = jnp.zeros(l_sc.shape, F32)
        acc_sc[...] = jnp.zeros(acc_sc.shape, F32)

    for cp in chunk_copies(t, slot):
        cp.wait()
    nxt_slot = lax.rem(t + DECODE_SLOTS - 1, DECODE_SLOTS)
    start_all(chunk_copies(jnp.minimum(t + DECODE_SLOTS - 1, last), nxt_slot))

    ql = ql_ref[...].reshape(rows, MLA_KV_RANK).astype(BF16)
    qr = qr_ref[...].reshape(rows, MLA_ROPE).astype(BF16)

    def attend(parts, mask=None):
        ss = []
        for lat_b, krt_b, irt in parts:
            s = _dot_nt(ql, lat_b) + _dot(qr, krt_b)
            s = (s.reshape(MLA_HEADS, t_new, -1) * irt[:, None, :]).reshape(rows, -1)
            ss.append(s if mask is None else jnp.where(mask, s, NEG_INF))
        m_prev = m_sc[...]
        m_new = functools.reduce(jnp.maximum, [jnp.max(s, -1, keepdims=True) for s in ss], m_prev)
        alpha = jnp.exp(m_prev - m_new)
        l = alpha * l_sc[...]
        acc = alpha * acc_sc[...]
        for s, (lat_b, _, _) in zip(ss, parts):
            p = jnp.exp(s - m_new)
            l = l + jnp.sum(p, -1, keepdims=True)
            acc = acc + _dot(p.astype(BF16), lat_b)
        l_sc[...] = l
        acc_sc[...] = acc
        m_sc[...] = m_new

    n_parts = min(DECODE_PARTS, pages)
    pp = pages // n_parts
    attend([(lat_buf[slot, i * pp * page:(i + 1) * pp * page, :].astype(BF16),
             jnp.concatenate([krt_buf[slot, p] for p in range(i * pp, (i + 1) * pp)], 1).astype(BF16),
             jnp.concatenate([irt_buf[slot, p] for p in range(i * pp, (i + 1) * pp)], 1)) for i in range(n_parts)])

    @pl.when(c == n_chunks - 1)
    def _():
        pad = 128 - t_new
        lb = jnp.concatenate([cn_ref[...], jnp.zeros((pad, MLA_KV_RANK), F32)], 0).astype(BF16)
        kb = jnp.concatenate([krn_ref[...], jnp.zeros((pad, MLA_ROPE), F32)], 0).T.astype(BF16)
        ib = jnp.concatenate([irn_ref[...], jnp.zeros((pad, 128), F32)], 0).T[:MLA_HEADS]
        tq = lax.rem(lax.broadcasted_iota(jnp.int32, (rows, 128), 0), t_new)
        key = lax.broadcasted_iota(jnp.int32, (rows, 128), 1)
        attend([(lb, kb, ib)], mask=key <= tq)
        o_lat = acc_sc[...] / l_sc[...]
        for h in range(MLA_HEADS):
            o_ref[:, h * MLA_V:(h + 1) * MLA_V] = _dot(o_lat[h * t_new:(h + 1) * t_new].astype(BF16), wuv_ref[h])

    @pl.when(t == last)
    def _():
        for ahead in range(1, DECODE_SLOTS):
            for cp in chunk_copies(last, lax.rem(t + ahead, DECODE_SLOTS)):
                cp.wait()


def _decode(page_table, ql, qr, c_new, kr_new, ir_new, wuv_h, lat_pool, krt_pool, irt_pool, pages):
    bs, n_pages = page_table.shape
    t_new = c_new.shape[1]
    page = lat_pool.shape[1]
    n_chunks = n_pages // pages
    keys = pages * page
    rows = MLA_HEADS * t_new
    per_b = lambda shape: pl.BlockSpec((None,) + shape, lambda t, pt: (t // n_chunks,) + (0,) * len(shape))
    hbm = pl.BlockSpec(memory_space=pl.ANY)
    in_specs = [pl.BlockSpec((MLA_HEADS, None, t_new, MLA_KV_RANK), lambda t, pt: (0, t // n_chunks, 0, 0)),
                pl.BlockSpec((MLA_HEADS, None, t_new, MLA_ROPE), lambda t, pt: (0, t // n_chunks, 0, 0)),
                per_b((t_new, MLA_KV_RANK)), per_b((t_new, MLA_ROPE)), per_b((t_new, 128)),
                pl.BlockSpec(wuv_h.shape, lambda t, pt: (0, 0, 0)), hbm, hbm, hbm]
    grid_spec = pltpu.PrefetchScalarGridSpec(
        num_scalar_prefetch=1,
        grid=(bs * n_chunks,),
        in_specs=in_specs,
        out_specs=per_b((t_new, MLA_HEADS * MLA_V)),
        scratch_shapes=[pltpu.VMEM((rows, 1), F32), pltpu.VMEM((rows, 1), F32), pltpu.VMEM((rows, MLA_KV_RANK), F32),
                        pltpu.VMEM((DECODE_SLOTS, keys, MLA_KV_RANK), F32),
                        pltpu.VMEM((DECODE_SLOTS, pages, MLA_ROPE, page), F32),
                        pltpu.VMEM((DECODE_SLOTS, pages, MLA_HEADS, page), F32),
                        pltpu.SemaphoreType.DMA((DECODE_SLOTS, 3))],
    )
    return pl.pallas_call(
        functools.partial(_decode_body, pages=pages, n_chunks=n_chunks, t_new=t_new),
        grid_spec=grid_spec,
        out_shape=jax.ShapeDtypeStruct((bs, t_new, MLA_HEADS * MLA_V), F32),
        compiler_params=_cparams("arbitrary"),
        name="mla_sample_attn",
    )(page_table.reshape(-1), ql, qr, c_new, kr_new, ir_new, wuv_h, lat_pool, krt_pool, irt_pool)


def _gla_body(gq_ref, gk_ref, gv_ref, r_ref, alr_ref, wga_ref, bga_ref, gn_ref, *rest, chunk, sub, nb, carry):
    if carry:
        y_ref, s_out_ref = rest
        s_in_ref = s_out_ref

        @pl.when(pl.program_id(0) == 0)
        def _():
            s_out_ref[...] = jnp.zeros(s_out_ref.shape, F32)
    else:
        s_in_ref, y_ref, s_out_ref = rest
    row = lax.broadcasted_iota(jnp.int32, (chunk, chunk), 0)
    col = lax.broadcasted_iota(jnp.int32, (chunk, chunk), 1)
    tri = (col <= row).astype(F32)
    ones_cv = jnp.ones((chunk, GLA_DV), F32)
    brow = lax.broadcasted_iota(jnp.int32, (sub, chunk), 0)
    bcol = lax.broadcasted_iota(jnp.int32, (sub, chunk), 1)
    hi = lax.Precision.HIGHEST
    gn = gn_ref[...]

    def group(bis):
        chains = [(n, h) for n in range(len(bis)) for h in range(GLA_HEADS)]
        ks = lambda h: slice(h * GLA_DK, (h + 1) * GLA_DK)
        vs = lambda h: slice(h * GLA_DV, (h + 1) * GLA_DV)
        gate = []
        for bi in bis:
            a = _dot(alr_ref[bi].astype(BF16), wga_ref[...]) + bga_ref[...]
            gate.append((jnp.minimum(a, 0.0) - jnp.log1p(jnp.exp(-jnp.abs(a)))) / GLA_TAU)
        cum = [_dot(tri, g, precision=hi) for g in gate]
        q = {(n, h): gq_ref[bis[n], :, ks(h)] * GLA_DK ** -0.5 for n, h in chains}
        k = {(n, h): gk_ref[bis[n], :, ks(h)] for n, h in chains}
        vb = {(n, h): gv_ref[bis[n], :, vs(h)].astype(BF16) for n, h in chains}
        s0 = {(n, h): s_in_ref[bis[n], h] for n, h in chains}
        bc = {(n, h): cum[n][:, ks(h)] for n, h in chains}
        o = {ch: _dot((q[ch] * jnp.exp(bc[ch])).astype(BF16), s0[ch].astype(BF16)) for ch in chains}
        for n, h in chains:
            ch = (n, h)
            dec = jnp.exp(_dot_tn(gate[n][:, ks(h)], ones_cv, precision=hi))
            kd = (k[ch] * jnp.exp(bc[ch][chunk - 1:chunk] - bc[ch])).astype(BF16)
            s_out_ref[bis[n], h] = dec * s0[ch] + _dot_tn(kd, vb[ch])
        for blk in range(chunk // sub):
            r0 = blk * sub
            a_blk = {}
            for ch in chains:
                qb = q[ch][r0:r0 + sub]
                bb = bc[ch][r0:r0 + sub]
                ab = jnp.zeros((sub, chunk), F32)
                for s in range(sub):
                    e = jnp.exp(jnp.minimum(bb - bb[s:s + 1], 0.0))
                    cs = jnp.sum(qb * e * k[ch][r0 + s:r0 + s + 1], -1, keepdims=True)
                    ab = jnp.where(bcol == r0 + s, jnp.where(brow >= s, cs, 0.0), ab)
                if blk > 0:
                    b0 = bc[ch][r0 - 1:r0]
                    qs = (qb * jnp.exp(bb - b0)).astype(BF16)
                    ksc = (k[ch] * jnp.exp(jnp.minimum(b0 - bc[ch], 0.0))).astype(BF16)
                    ab = jnp.where(bcol < r0, _dot_nt(qs, ksc), ab)
                a_blk[ch] = ab
            for n, h in chains:
                ch = (n, h)
                ob = o[ch][r0:r0 + sub] + _dot(a_blk[ch].astype(BF16), vb[ch])
                rr = r_ref[bis[n], r0:r0 + sub, vs(h)]
                y_ref[bis[n], r0:r0 + sub, vs(h)] = _rms(ob, gn) * (rr * jax.nn.sigmoid(rr))

    if carry:
        group(list(range(nb)))
    else:
        def rows(i, carry_):
            group([i * GLA_ROWS_PER_ITER + j for j in range(GLA_ROWS_PER_ITER)])
            return carry_

        lax.fori_loop(0, nb // GLA_ROWS_PER_ITER, rows, 0)


def _gla(z3, state, wga, bga, gn, chunk, sub, nb):
    b, s, _ = z3.shape
    carry = state is None
    if carry:
        grid = (s // chunk,)
        zmap = lambda blk: (lambda c: (0, c, blk))
        smap = lambda c: (0, 0, 0, 0)
        nb = b
    else:
        grid = (b // nb,)
        zmap = lambda blk: (lambda i: (i, 0, blk))
        smap = lambda i: (i, 0, 0, 0)
    zspec = lambda wb: pl.BlockSpec((nb, chunk, wb[0]), zmap(wb[1]))
    const = lambda shape: pl.BlockSpec(shape, lambda i: (0,) * len(shape))
    sspec = pl.BlockSpec((nb, GLA_HEADS, GLA_DK, GLA_DV), smap)
    in_specs = [zspec(ZB_GQ), zspec(ZB_GK), zspec(ZB_GV), zspec(ZB_R), zspec(ZB_ALR),
                const(wga.shape), const(bga.shape), const(gn.shape)]
    args = [z3, z3, z3, z3, z3, wga, bga, gn]
    if not carry:
        in_specs.append(sspec)
        args.append(state)
    return pl.pallas_call(
        functools.partial(_gla_body, chunk=chunk, sub=sub, nb=nb, carry=carry),
        grid=grid,
        in_specs=in_specs,
        out_specs=[pl.BlockSpec((nb, chunk, GLA_HEADS * GLA_DV), zmap(0)), sspec],
        out_shape=[jax.ShapeDtypeStruct((b, s, GLA_HEADS * GLA_DV), F32),
                   jax.ShapeDtypeStruct((b, GLA_HEADS, GLA_DK, GLA_DV), F32)],
        compiler_params=_cparams("arbitrary"),
        name="gla_prompt" if carry else "gla_sample",
    )(*args)


def _out_proj_body(mla_ref, gla_ref, x_ref, nm_ref, w_ref, o_ref, *, mla_feature_major):
    mla = mla_ref[...].T if mla_feature_major else mla_ref[...]
    half = mla.shape[1]
    mn = _rms(mla, nm_ref[...]).astype(BF16)
    o_ref[...] = x_ref[...] + _dot(mn, w_ref[:half]) + _dot(gla_ref[...].astype(BF16), w_ref[half:])


def _out_proj(mla_o, gla_y, x, nm, w, tm):
    n, d = x.shape
    half = gla_y.shape[1]
    feature_major = mla_o.ndim == 3
    if feature_major:
        per_b = mla_o.shape[2] // tm
        mla_spec = pl.BlockSpec((None, half, tm), lambda i: (i // per_b, 0, lax.rem(i, per_b)))
    else:
        mla_spec = pl.BlockSpec((tm, half), lambda i: (i, 0))
    return pl.pallas_call(
        functools.partial(_out_proj_body, mla_feature_major=feature_major),
        grid=(n // tm,),
        in_specs=[mla_spec, pl.BlockSpec((tm, half), lambda i: (i, 0)),
                  pl.BlockSpec((tm, d), lambda i: (i, 0)), pl.BlockSpec((1, half), lambda i: (0, 0)),
                  pl.BlockSpec(w.shape, lambda i: (0, 0))],
        out_specs=pl.BlockSpec((tm, d), lambda i: (i, 0)),
        out_shape=jax.ShapeDtypeStruct((n, d), F32),
        compiler_params=_cparams("parallel"),
        name="out_proj",
    )(mla_o, gla_y, x, nm.reshape(1, half), w)


def _top_desc(s, n):
    vals = []
    w = s
    for _ in range(n):
        m = jnp.max(w, axis=0, keepdims=True)
        vals.append(m)
        w = jnp.where(w == m, NEG_INF, w)
    return vals


def _peer_route_body(h_ref, g_ref, wqt_ref, k1_ref, k2_ref, xnt_ref, s2_ref, e2_ref, thr_ref, e1z_ref):
    xnt = _rms(h_ref[...], g_ref[...]).T.astype(BF16)
    xnt_ref[...] = xnt
    qt = _dot(wqt_ref[...], xnt).astype(BF16)
    half = PEER_N_KEYS
    for h in range(PEER_HEADS):
        s1 = _dot(k1_ref[...], qt[(2 * h) * half:(2 * h + 1) * half])
        s2 = _dot(k2_ref[...], qt[(2 * h + 1) * half:(2 * h + 2) * half])
        v1 = _top_desc(s1, PEER_TOPK)
        v2 = _top_desc(s2, PEER_TOPK)
        pairs = [(a, b) for a in range(PEER_TOPK) for b in range(PEER_TOPK // (a + 1))]
        sums = {ab: v1[ab[0]] + v2[ab[1]] for ab in pairs}
        npad = -len(pairs) % 8
        stack = jnp.concatenate([sums[ab] for ab in pairs] + [jnp.full_like(v1[0], NEG_INF)] * npad, 0)
        tau = _top_desc(stack, PEER_TOPK)[-1]
        top = v1[0] + v2[0]
        zsum = jnp.sum(jnp.where(stack >= tau, jnp.exp(stack - top), 0.0), axis=0, keepdims=True)
        thr = jnp.full(s1.shape, POS_INF, F32)
        for a in range(PEER_TOPK):
            ta = jnp.full_like(tau, POS_INF)
            for b in range(PEER_TOPK // (a + 1)):
                ta = jnp.minimum(ta, jnp.where(sums[(a, b)] >= tau, v2[b], POS_INF))
            thr = jnp.where(s1 == v1[a], ta, thr)
        e2 = jnp.exp(s2 - v2[0])
        e1z = jnp.exp(s1 - v1[0]) / zsum
        for lg in range(thr.shape[1] // 128):
            lanes = slice(lg * 128, (lg + 1) * 128)
            s2_ref[h, lg] = s2[:, lanes]
            e2_ref[h, lg] = e2[:, lanes]
            thr_ref[h, lg] = thr[:, lanes]
            e1z_ref[h, lg] = e1z[:, lanes]


def _peer_route(h, g, wqt, k1, k2, tt):
    n, d = h.shape
    per_group = jax.ShapeDtypeStruct((PEER_HEADS, n // 128, PEER_N_KEYS, 128), F32)
    gspec = pl.BlockSpec((PEER_HEADS, tt // 128, PEER_N_KEYS, 128), lambda i: (0, i, 0, 0))
    return pl.pallas_call(
        _peer_route_body,
        grid=(n // tt,),
        in_specs=[pl.BlockSpec((tt, d), lambda i: (i, 0)), pl.BlockSpec((1, d), lambda i: (0, 0)),
                  pl.BlockSpec(wqt.shape, lambda i: (0, 0)), pl.BlockSpec(k1.shape, lambda i: (0, 0)),
                  pl.BlockSpec(k2.shape, lambda i: (0, 0))],
        out_specs=[pl.BlockSpec((d, tt), lambda i: (0, i)), gspec, gspec, gspec, gspec],
        out_shape=[jax.ShapeDtypeStruct((d, n), BF16), per_group, per_group, per_group, per_group],
        compiler_params=_cparams("parallel"),
        name="peer_route",
    )(h, g.reshape(1, d), wqt, k1, k2)


def _peer_dense_body(xnt_ref, u_ref, vt_ref, s2_ref, e2_ref, thr_ref, e1z_ref, yt_ref, act_sc, ca_sc):
    @pl.when(pl.program_id(1) == 0)
    def _():
        yt_ref[...] = jnp.zeros(yt_ref.shape, F32)

    n_lg, te, _ = act_sc.shape
    act = _dot(u_ref[...], xnt_ref[...])
    act = 0.5 * act * (1.0 + lax.erf(act * (2.0 ** -0.5)))
    for lg in range(n_lg):
        act_sc[lg] = act[:, lg * 128:(lg + 1) * 128]
    groups = te // PEER_N_KEYS
    n_sub = PEER_N_KEYS // PEER_SUB

    def gate_block(blk, carry):
        lg = blk // n_sub
        k0 = pl.multiple_of(lax.rem(blk, n_sub) * PEER_SUB, PEER_SUB)
        coef = [jnp.zeros((PEER_SUB, 128), F32)] * groups
        for h in range(PEER_HEADS):
            s2b = s2_ref[h, lg, pl.ds(k0, PEER_SUB), :]
            e2b = e2_ref[h, lg, pl.ds(k0, PEER_SUB), :]
            for gi in range(groups):
                thr = jnp.broadcast_to(thr_ref[h, lg, gi:gi + 1, :], (PEER_SUB, 128))
                e1z = jnp.broadcast_to(e1z_ref[h, lg, gi:gi + 1, :], (PEER_SUB, 128))
                coef[gi] = coef[gi] + jnp.where(s2b >= thr, e2b, 0.0) * e1z
        for gi in range(groups):
            rows = pl.ds(pl.multiple_of(gi * PEER_N_KEYS + k0, PEER_SUB), PEER_SUB)
            ca_sc[lg, rows, :] = (coef[gi] * act_sc[lg, rows, :]).astype(BF16)
        return carry

    lax.fori_loop(0, n_lg * n_sub, gate_block, 0)
    ca = jnp.concatenate([ca_sc[lg] for lg in range(n_lg)], 1)
    yt_ref[...] += _dot(vt_ref[...], ca)


def _peer_dense(xnt, u, vt, s2, e2, thr, e1z, tt, te):
    d, n = xnt.shape
    ne = u.shape[0] // te
    gpt = te // PEER_N_KEYS
    full = pl.BlockSpec((PEER_HEADS, tt // 128, PEER_N_KEYS, 128), lambda i, j: (0, i, 0, 0))
    part = pl.BlockSpec((PEER_HEADS, tt // 128, gpt, 128), lambda i, j: (0, i, j, 0))
    return pl.pallas_call(
        _peer_dense_body,
        grid=(n // tt, ne),
        in_specs=[pl.BlockSpec((d, tt), lambda i, j: (0, i)), pl.BlockSpec((te, d), lambda i, j: (j, 0)),
                  pl.BlockSpec((d, te), lambda i, j: (0, j)), full, full, part, part],
        out_specs=pl.BlockSpec((d, tt), lambda i, j: (0, i)),
        out_shape=jax.ShapeDtypeStruct((d, n), F32),
        scratch_shapes=[pltpu.VMEM((tt // 128, te, 128), F32), pltpu.VMEM((tt // 128, te, 128), BF16)],
        compiler_params=_cparams("parallel", "arbitrary"),
        name="peer_dense",
    )(xnt, u, vt, s2, e2, thr, e1z)


def _ple_body(h_ref, yt_ref, p_ref, g_ref, wg_ref, wp_ref, o_ref):
    h2 = h_ref[...] + yt_ref[...].T
    gate = jax.nn.sigmoid(_dot(_rms(h2, g_ref[...]).astype(BF16), wg_ref[...]))
    o_ref[...] = h2 + gate * _dot(p_ref[...].astype(BF16), wp_ref[...])


def _ple(h, yt, p, g, wg, wp, tm):
    n, d = h.shape
    return pl.pallas_call(
        _ple_body,
        grid=(n // tm,),
        in_specs=[pl.BlockSpec((tm, d), lambda i: (i, 0)), pl.BlockSpec((d, tm), lambda i: (0, i)),
                  pl.BlockSpec((tm, p.shape[1]), lambda i: (i, 0)), pl.BlockSpec((1, d), lambda i: (0, 0)),
                  pl.BlockSpec(wg.shape, lambda i: (0, 0)), pl.BlockSpec(wp.shape, lambda i: (0, 0))],
        out_specs=pl.BlockSpec((tm, d), lambda i: (i, 0)),
        out_shape=jax.ShapeDtypeStruct((n, d), F32),
        compiler_params=_cparams("parallel"),
        name="ple",
    )(h, yt, p, g.reshape(1, d), wg, wp)


def _rope_table(pos):
    half = MLA_ROPE // 2
    inv_freq = ROPE_THETA ** (-jnp.arange(half, dtype=F32) / half)
    ang = pos.astype(F32)[:, None] * inv_freq
    cos, sin = jnp.cos(ang), jnp.sin(ang)
    return jnp.concatenate([cos, cos, sin, sin], -1)


def _tile(n, t):
    return t if n % t == 0 else n


def _finish(x2, mla_o, gla_y, p2, w):
    n = x2.shape[0]
    h = _out_proj(mla_o, gla_y, x2, w["norm_mla_out"], w["w_out"], _tile(n, TM_OUT))
    xnt, s2, e2, thr, e1z = _peer_route(h, w["norm_ffn"], w["wqt"], w["k1"], w["k2"], _tile(n, TT_ROUTE))
    yt = _peer_dense(xnt, w["u"], w["vt"], s2, e2, thr, e1z, _tile(n, TT_DENSE), TE_DENSE)
    return _ple(h, yt, p2, w["norm_ple"], w["w_ple_gate"], w["w_ple_proj"], _tile(n, TM_PLE))


def kernel(x_prompt, x_sample, cache_kv_latent, cache_k_rope, cache_k_inv_rms, state_gla, page_table, p_prompt,
           p_sample, norm_mix, w_in, norm_q_lat, w_uq, norm_kv_lat, w_uk, w_uv, qk_gain_q, qk_gain_k, norm_mla_out,
           w_gla_a, b_gla_a, norm_gla_out, w_out, norm_ffn, peer_w_q, peer_keys1, peer_keys2, peer_u, peer_v,
           norm_ple, w_ple_gate, w_ple_proj):
    depth = w_in.shape[0]
    b, s, d = x_prompt.shape
    bs, t_new, _ = x_sample.shape
    n_pages = page_table.shape[1]
    page = cache_kv_latent.shape[2]
    past_len = n_pages * page
    tab_p = _rope_table(jnp.arange(s))
    tab_s = _rope_table(past_len + jnp.arange(t_new))
    hp, hs = x_prompt.reshape(b * s, d), x_sample.reshape(bs * t_new, d)
    outs = [[] for _ in range(8)]
    for i in range(depth):
        wi = w_in[i]
        cuts = [0, 512, 768, 832, 1344, 1856, 2880, 3904, 3920]
        cq_w, ckv_w, kr_w, gq_w, gk_w, gv_w, r_w, alr_w = [wi[:, cuts[j]:cuts[j + 1]] for j in range(8)]
        w_in_ext = jnp.concatenate(
            [cq_w, gq_w, gk_w, ckv_w, kr_w, _rot_cols(kr_w), alr_w,
             jnp.zeros((d, 128 - GLA_GATE_RANK), F32), gv_w, r_w], 1).astype(BF16)
        wuq3 = w_uq[i].reshape(MLA_Q_RANK, MLA_HEADS, MLA_QK)
        wuq_ext = jnp.concatenate([wuq3, _rot_cols(wuq3[..., MLA_NOPE:])], -1).reshape(MLA_Q_RANK, -1).astype(BF16)
        gq, gk = qk_gain_q[i], qk_gain_k[i]
        gq_ext = jnp.concatenate([gq, _rot_gain(gq[MLA_NOPE:])]).reshape(1, -1)
        gk_ext = jnp.concatenate([gk, _rot_gain(gk[MLA_NOPE:])]).reshape(1, -1)
        wuk = w_uk[i].reshape(MLA_KV_RANK, -1).astype(BF16)
        wuv = w_uv[i].reshape(MLA_KV_RANK, -1).astype(BF16)
        wuk_t = jnp.transpose(w_uk[i], (1, 2, 0)).astype(BF16)
        wuv_h = jnp.transpose(w_uv[i], (1, 0, 2)).astype(BF16)
        wga = jnp.concatenate([w_gla_a[i], jnp.zeros((128 - GLA_GATE_RANK, w_gla_a.shape[2]), F32)], 0).astype(BF16)
        bga = b_gla_a[i].reshape(1, -1)
        gn = norm_gla_out[i].reshape(1, -1)
        w = {"norm_mla_out": norm_mla_out[i], "w_out": w_out[i].astype(BF16), "norm_ffn": norm_ffn[i],
             "wqt": peer_w_q[i].T.astype(BF16), "k1": peer_keys1[i].astype(BF16), "k2": peer_keys2[i].astype(BF16),
             "u": peer_u[i].astype(BF16), "vt": peer_v[i].T.astype(BF16), "norm_ple": norm_ple[i],
             "w_ple_gate": w_ple_gate[i].astype(BF16), "w_ple_proj": w_ple_proj[i].astype(BF16)}
        prep = (norm_q_lat[i], norm_kv_lat[i], gq_ext, gk_ext, wuq_ext, wuk)

        n = b * s
        z = _norm_matmul(hp, norm_mix[i], w_in_ext, _tile(n, TM_IN), TN_IN)
        tm = _tile(s, TM_PREP)
        qt, kt, vt, lat, kr, ir = _mla_prep(z, tab_p, s // tm, *prep, wuv, tm, sample=False)
        tq = _tile(s, FLASH_TILE)
        mla_o = _flash(qt.reshape(b, s, -1), kt.reshape(b, s, -1), vt, tq, FLASH_HEADS)
        gla_y, s_fin = _gla(z.reshape(b, s, Z_COLS), None, wga, bga, gn, GLA_CHUNK, GLA_SUB, b)
        hp_next = _finish(hp, mla_o, gla_y.reshape(n, -1), p_prompt[i].reshape(n, -1), w)
        outs[0].append(lat.reshape(b, s, -1))
        outs[1].append(kr.reshape(b, s, -1))
        outs[2].append(ir[:, :MLA_HEADS].reshape(b, s, -1))
        outs[3].append(s_fin)

        n = bs * t_new
        z = _norm_matmul(hs, norm_mix[i], w_in_ext, _tile(n, TM_IN), TN_IN)
        tm = _tile(n, TM_PREP)
        tab = jnp.tile(tab_s, (tm // t_new, 1))
        ql, qr, lat, kr, ir = _mla_prep(z, tab, 1, *prep, wuk_t, tm, sample=True)
        mla_o = _decode(page_table, ql.reshape(MLA_HEADS, bs, t_new, -1), qr.reshape(MLA_HEADS, bs, t_new, -1),
                        lat.reshape(bs, t_new, -1), kr.reshape(bs, t_new, -1), ir.reshape(bs, t_new, -1), wuv_h,
                        cache_kv_latent[i], jnp.transpose(cache_k_rope[i], (0, 2, 1)),
                        jnp.transpose(cache_k_inv_rms[i], (0, 2, 1)), _tile(n_pages, DECODE_PAGES))
        gla_y, s_new = _gla(z.reshape(bs, t_new, Z_COLS), state_gla[i], wga, bga, gn, t_new, t_new,
                            _tile(bs, GLA_SAMPLE_ROWS))
        hs_next = _finish(hs, mla_o.reshape(n, -1), gla_y.reshape(n, -1), p_sample[i].reshape(n, -1), w)
        outs[4].append(lat.reshape(bs, t_new, -1))
        outs[5].append(kr.reshape(bs, t_new, -1))
        outs[6].append(ir[:, :MLA_HEADS].reshape(bs, t_new, -1))
        outs[7].append(s_new)
        hp, hs = hp_next, hs_next
    return (hp.reshape(b, s, d), hs.reshape(bs, t_new, d)) + tuple(jnp.stack(o, 0) for o in outs)
```

```python
import functools

import jax
import jax.numpy as jnp
from jax import lax
from jax.experimental import pallas as pl
from jax.experimental.pallas import tpu as pltpu

F32 = jnp.float32
BF16 = jnp.bfloat16
EPS = 1e-6
NEG_INF = float("-inf")
POS_INF = float("inf")

MLA_HEADS = 8
MLA_NOPE = 128
MLA_ROPE = 64
MLA_QK = MLA_NOPE + MLA_ROPE
MLA_V = 128
MLA_Q_RANK = 512
MLA_KV_RANK = 256
MLA_SCALE = MLA_QK ** -0.5
LOG2E = 1.4426950408889634
MLA_HEAD_PAD = 256
ROPE_THETA = 10000.0
GLA_HEADS = 4
GLA_DK = 128
GLA_DV = 256
GLA_GATE_RANK = 16
GLA_TAU = 16.0
GLA_CHUNK = 64
GLA_SUB = 16
PEER_HEADS = 8
PEER_N_KEYS = 128
PEER_TOPK = 16
PEER_SUB = 32
PLE_DIM = 256

Z_COLS = 4096
ZB_CQ = (512, 0)
ZB_GQ = (512, 1)
ZB_GK = (512, 2)
ZB_CKV = (256, 6)
ZB_KR = (128, 14)
ZB_ALR = (128, 15)
ZB_GV = (1024, 2)
ZB_R = (1024, 3)

V7X_VMEM_LIMIT = 56 * 1024 * 1024

TM_IN, TN_IN = 1024, 1024
TM_PREP = 256
FLASH_TILE = 512
FLASH_HEADS = 8
FLASH_SUB = 512
DECODE_PAGES = 64
DECODE_PARTS = 8
DECODE_SLOTS = 3
GLA_SAMPLE_ROWS = 8
GLA_ROWS_PER_ITER = 2
TM_OUT = 256
TT_ROUTE = 256
TT_DENSE, TE_DENSE = 512, 1024
TM_PLE = 256


def _cparams(*sem):
    return pltpu.CompilerParams(dimension_semantics=sem, vmem_limit_bytes=V7X_VMEM_LIMIT)


def _dot(a, b, **kw):
    return jnp.dot(a, b, preferred_element_type=F32, **kw)


def _dot_nt(a, b):
    return lax.dot_general(a, b, (((1,), (1,)), ((), ())), preferred_element_type=F32)


def _dot_tn(a, b, **kw):
    return lax.dot_general(a, b, (((0,), (0,)), ((), ())), preferred_element_type=F32, **kw)


def _rms(x, g):
    return x * lax.rsqrt(jnp.mean(x * x, -1, keepdims=True) + EPS) * g


def _rot_cols(w):
    h = w.shape[-1] // 2
    return jnp.concatenate([-w[..., h:], w[..., :h]], -1)


def _rot_gain(g):
    h = g.shape[-1] // 2
    return jnp.concatenate([g[..., h:], g[..., :h]], -1)


def _norm_matmul_body(x_ref, g_ref, w_ref, o_ref, xn_sc):
    @pl.when(pl.program_id(1) == 0)
    def _():
        xn_sc[...] = _rms(x_ref[...], g_ref[...]).astype(BF16)

    o_ref[...] = _dot(xn_sc[...], w_ref[...])


def _norm_matmul(x, g, w, tm, tn):
    n, d = x.shape
    nc = w.shape[1]
    return pl.pallas_call(
        _norm_matmul_body,
        grid=(n // tm, nc // tn),
        in_specs=[pl.BlockSpec((tm, d), lambda i, j: (i, 0)),
                  pl.BlockSpec((1, d), lambda i, j: (0, 0)),
                  pl.BlockSpec((d, tn), lambda i, j: (0, j))],
        out_specs=pl.BlockSpec((tm, tn), lambda i, j: (i, j)),
        out_shape=jax.ShapeDtypeStruct((n, nc), F32),
        scratch_shapes=[pltpu.VMEM((tm, d), BF16)],
        compiler_params=_cparams("parallel", "arbitrary"),
        name="in_proj",
    )(x, g.reshape(1, d), w)


def _mla_prep_body(cq_ref, ckv_ref, krb_ref, tab_ref, nq_ref, nkv_ref, gq_ref, gk_ref, wuq_ref, wuk_ref, wx_ref,
                   *outs, sample):
    if sample:
        ql_ref, qr_ref, lat_ref, kr_ref, ir_ref = outs
    else:
        qt_ref, kt_ref, vt_ref, lat_ref, kr_ref, ir_ref = outs
    tab = tab_ref[...]
    gq = gq_ref[...]
    gk = gk_ref[...]
    cqn = _rms(cq_ref[...], nq_ref[...]).astype(BF16)
    qraw = _dot(cqn, wuq_ref[...])
    c = _rms(ckv_ref[...], nkv_ref[...])
    lat_ref[...] = c
    cb = c.astype(BF16)
    kn = _dot(cb, wuk_ref[...])
    if not sample:
        vt_ref[...] = _dot(cb, wx_ref[...]).T.astype(BF16)
    krb = krb_ref[...]
    lane = lax.broadcasted_iota(jnp.int32, krb.shape, 1)
    ss_kr = jnp.sum(jnp.where(lane < MLA_ROPE, krb * krb, 0.0), -1, keepdims=True)
    t = krb * gk[:, MLA_NOPE:] * tab
    kr2 = t + pltpu.roll(t, MLA_ROPE, 1)
    kr_ref[...] = kr2[:, :MLA_ROPE]
    krz = jnp.where(lane < MLA_ROPE, kr2, 0.0)
    lane_q = lax.broadcasted_iota(jnp.int32, (krb.shape[0], MLA_HEAD_PAD), 1)
    ir_acc = jnp.zeros(krb.shape, F32)
    for h in range(MLA_HEADS):
        knh = kn[:, h * MLA_NOPE:(h + 1) * MLA_NOPE]
        ss = jnp.sum(knh * knh, -1, keepdims=True) + ss_kr
        inv = lax.rsqrt(ss / MLA_QK + EPS)
        ir_acc = jnp.where(lane == h, inv, ir_acc)
        qh = qraw[:, h * MLA_HEAD_PAD:(h + 1) * MLA_HEAD_PAD]
        ssq = jnp.sum(jnp.where(lane_q < MLA_QK, qh * qh, 0.0), -1, keepdims=True)
        qinv = lax.rsqrt(ssq / MLA_QK + EPS)
        qn = qh[:, :MLA_NOPE] * qinv * gq[:, :MLA_NOPE]
        tq = qh[:, MLA_NOPE:] * qinv * gq[:, MLA_NOPE:] * tab
        qr2 = tq + pltpu.roll(tq, MLA_ROPE, 1)
        if sample:
            qg = (qn * gk[:, :MLA_NOPE]).astype(BF16)
            ql_ref[h] = _dot(qg, wx_ref[h]) * MLA_SCALE
            qr_ref[h] = qr2[:, :MLA_ROPE] * MLA_SCALE
        else:
            lo = h * MLA_HEAD_PAD
            qt_ref[:, lo:lo + MLA_NOPE] = (qn * (MLA_SCALE * LOG2E)).astype(BF16)
            qt_ref[:, lo + MLA_NOPE:lo + MLA_HEAD_PAD] = (qr2 * (MLA_SCALE * LOG2E)).astype(BF16)
            kt_ref[:, lo:lo + MLA_NOPE] = (knh * gk[:, :MLA_NOPE] * inv).astype(BF16)
            kt_ref[:, lo + MLA_NOPE:lo + MLA_HEAD_PAD] = (krz * inv).astype(BF16)
    ir_ref[...] = ir_acc


def _mla_prep(z, tab, tab_blocks, nq, nkv, gq_ext, gk_ext, wuq_ext, wuk, wx, tm, sample):
    n = z.shape[0]
    hp = MLA_HEADS * MLA_HEAD_PAD

    def const(shape):
        return pl.BlockSpec(shape, lambda i: (0,) * len(shape))

    in_specs = [pl.BlockSpec((tm, ZB_CQ[0]), lambda i: (i, ZB_CQ[1])),
                pl.BlockSpec((tm, ZB_CKV[0]), lambda i: (i, ZB_CKV[1])),
                pl.BlockSpec((tm, ZB_KR[0]), lambda i: (i, ZB_KR[1])),
                pl.BlockSpec((tm, 128), lambda i: (i % tab_blocks, 0)),
                const((1, MLA_Q_RANK)), const((1, MLA_KV_RANK)), const((1, MLA_HEAD_PAD)), const((1, MLA_HEAD_PAD)),
                const(wuq_ext.shape), const(wuk.shape), const(wx.shape)]
    row = lambda w: pl.BlockSpec((tm, w), lambda i: (i, 0))
    tail_specs = [row(MLA_KV_RANK), row(MLA_ROPE), row(128)]
    tail_shapes = [jax.ShapeDtypeStruct((n, MLA_KV_RANK), F32), jax.ShapeDtypeStruct((n, MLA_ROPE), F32),
                   jax.ShapeDtypeStruct((n, 128), F32)]
    if sample:
        head = lambda w: pl.BlockSpec((MLA_HEADS, tm, w), lambda i: (0, i, 0))
        out_specs = [head(MLA_KV_RANK), head(MLA_ROPE)] + tail_specs
        out_shape = [jax.ShapeDtypeStruct((MLA_HEADS, n, MLA_KV_RANK), F32),
                     jax.ShapeDtypeStruct((MLA_HEADS, n, MLA_ROPE), F32)] + tail_shapes
    else:
        out_specs = [row(hp), row(hp), pl.BlockSpec((MLA_HEADS * MLA_V, tm), lambda i: (0, i))] + tail_specs
        out_shape = [jax.ShapeDtypeStruct((n, hp), BF16), jax.ShapeDtypeStruct((n, hp), BF16),
                     jax.ShapeDtypeStruct((MLA_HEADS * MLA_V, n), BF16)] + tail_shapes
    return pl.pallas_call(
        functools.partial(_mla_prep_body, sample=sample),
        grid=(n // tm,),
        in_specs=in_specs, out_specs=out_specs, out_shape=out_shape,
        compiler_params=_cparams("parallel"),
        name="mla_prep_sample" if sample else "mla_prep_prompt",
    )(z, z, z, tab, nq.reshape(1, -1), nkv.reshape(1, -1), gq_ext, gk_ext, wuq_ext, wuk, wx)


def _flash_body(q_ref, k_ref, vt_ref, o_ref, m_sc, l_sc, acc_sc, *, heads, sub):
    i = pl.program_id(2)
    j = pl.program_id(3)

    @pl.when(j == 0)
    def _():
        m_sc[...] = jnp.full(m_sc.shape, NEG_INF, F32)
        l_sc[...] = jnp.zeros(l_sc.shape, F32)
        acc_sc[...] = jnp.zeros(acc_sc.shape, F32)

    tile = q_ref.shape[0]

    def step(diagonal):
        def q_cols(c, carry):
            r0 = pl.multiple_of(c * sub, sub)
            cols = pl.ds(r0, sub)
            vrows = [slice(h * MLA_V, (h + 1) * MLA_V) for h in range(heads)]
            prev = [(m_sc[h, :, cols], l_sc[h, :, cols], acc_sc[vrows[h], cols]) for h in range(heads)]
            new = []

            def scores(h):
                qk = slice(h * MLA_HEAD_PAD, (h + 1) * MLA_HEAD_PAD)
                return _dot_nt(k_ref[:, qk], q_ref[cols, qk])

            ahead = 3
            pending = [scores(h) for h in range(min(ahead, heads))]
            for h in range(heads):
                s = pending.pop(0)
                if diagonal:
                    keep = (lax.broadcasted_iota(jnp.int32, s.shape, 0)
                            <= r0 + lax.broadcasted_iota(jnp.int32, s.shape, 1))
                    s = jnp.where(keep, s, NEG_INF)
                m_prev, l_prev, acc_prev = prev[h]
                m_new = jnp.maximum(m_prev, jnp.max(s, 0, keepdims=True))
                alpha = jnp.exp2(m_prev - m_new)
                p = jnp.exp2(s - m_new)
                l_new = alpha * l_prev + jnp.sum(p, 0, keepdims=True)
                acc = alpha * acc_prev + _dot(vt_ref[vrows[h], :], p.astype(BF16))
                new.append((m_new, l_new, acc))
                if h + ahead < heads:
                    pending.append(scores(h + ahead))
            for h, (m_new, l_new, acc) in enumerate(new):
                if diagonal:
                    o_ref[vrows[h], cols] = acc / l_new
                else:
                    m_sc[h, :, cols] = m_new
                    l_sc[h, :, cols] = l_new
                    acc_sc[vrows[h], cols] = acc
            return carry

        lax.fori_loop(0, tile // sub, q_cols, 0)

    pl.when(j < i)(lambda: step(False))
    pl.when(j == i)(lambda: step(True))


def _flash(qt, kt, vt, tile, heads):
    b, s, _ = qt.shape
    nt = s // tile
    return pl.pallas_call(
        functools.partial(_flash_body, heads=heads, sub=min(FLASH_SUB, tile)),
        grid=(b, MLA_HEADS // heads, nt, nt),
        in_specs=[pl.BlockSpec((None, tile, heads * MLA_HEAD_PAD), lambda bb, h, i, j: (bb, i, h)),
                  pl.BlockSpec((None, tile, heads * MLA_HEAD_PAD), lambda bb, h, i, j: (bb, jnp.minimum(j, i), h)),
                  pl.BlockSpec((heads * MLA_V, tile), lambda bb, h, i, j: (h, bb * nt + jnp.minimum(j, i)))],
        out_specs=pl.BlockSpec((None, heads * MLA_V, tile), lambda bb, h, i, j: (bb, h, i)),
        out_shape=jax.ShapeDtypeStruct((b, MLA_HEADS * MLA_V, s), F32),
        scratch_shapes=[pltpu.VMEM((heads, 1, tile), F32), pltpu.VMEM((heads, 1, tile), F32),
                        pltpu.VMEM((heads * MLA_V, tile), F32)],
        compiler_params=_cparams("parallel", "parallel", "parallel", "arbitrary"),
        name="mla_prompt_attn",
    )(qt, kt, vt)


def _decode_body(pt_ref, ql_ref, qr_ref, cn_ref, krn_ref, irn_ref, wuv_ref, lat_pool, krt_pool, irt_pool,
                 o_ref, m_sc, l_sc, acc_sc, lat_buf, krt_buf, irt_buf, sems, *, pages, n_chunks, t_new):
    rows = MLA_HEADS * t_new
    page = lat_buf.shape[1] // pages
    t = pl.program_id(0)
    last = pl.num_programs(0) - 1
    c = lax.rem(t, n_chunks)
    slot = lax.rem(t, DECODE_SLOTS)

    def chunk_copies(step, buf_slot):
        copies = []
        for p in range(pages):
            pg = pt_ref[step * pages + p]
            keys = pl.ds(p * page, page)
            copies.append(pltpu.make_async_copy(lat_pool.at[pg], lat_buf.at[buf_slot, keys, :], sems.at[buf_slot, 0]))
            copies.append(pltpu.make_async_copy(krt_pool.at[pg], krt_buf.at[buf_slot, p], sems.at[buf_slot, 1]))
            copies.append(pltpu.make_async_copy(irt_pool.at[pg], irt_buf.at[buf_slot, p], sems.at[buf_slot, 2]))
        return copies

    @pl.when(t == 0)
    def _():
        for ahead in range(DECODE_SLOTS - 1):
            for cp in chunk_copies(jnp.minimum(ahead, last), ahead):
                cp.start()

    @pl.when(c == 0)
    def _():
        m_sc[...] = jnp.full(m_sc.shape, NEG_INF, F32)
        l_sc[...] = jnp.zeros(l_sc.shape, F32)
        acc_sc[...] = jnp.zeros(acc_sc.shape, F32)

    for cp in chunk_copies(t, slot):
        cp.wait()
    nxt_slot = lax.rem(t + DECODE_SLOTS - 1, DECODE_SLOTS)
    ql = ql_ref[...].reshape(rows, MLA_KV_RANK).astype(BF16)
    qr = qr_ref[...].reshape(rows, MLA_ROPE).astype(BF16)

    def attend(parts, masks):
        ss = []
        for (lat_b, krt_b, irt), mask in zip(parts, masks):
            s = _dot_nt(ql, lat_b) + _dot(qr, krt_b)
            s = (s.reshape(MLA_HEADS, t_new, -1) * irt[:, None, :]).reshape(rows, -1)
            ss.append(s if mask is None else jnp.where(mask, s, NEG_INF))
        m_prev = m_sc[...]
        m_new = functools.reduce(jnp.maximum, [jnp.max(s, -1, keepdims=True) for s in ss], m_prev)
        alpha = jnp.exp(m_prev - m_new)
        l = alpha * l_sc[...]
        acc = alpha * acc_sc[...]
        for s, (lat_b, _, _) in zip(ss, parts):
            p = jnp.exp(s - m_new)
            l = l + jnp.sum(p, -1, keepdims=True)
            acc = acc + _dot(p.astype(BF16), lat_b)
        return m_new, l, acc

    n_parts = min(DECODE_PARTS, pages)
    pp = pages // n_parts

    def step(with_new_tokens):
        for cp in chunk_copies(jnp.minimum(t + DECODE_SLOTS - 1, last), nxt_slot):
            cp.start()
        parts = [(lat_buf[slot, i * pp * page:(i + 1) * pp * page, :].astype(BF16),
                  jnp.concatenate([krt_buf[slot, p] for p in range(i * pp, (i + 1) * pp)], 1).astype(BF16),
                  jnp.concatenate([irt_buf[slot, p] for p in range(i * pp, (i + 1) * pp)], 1))
                 for i in range(n_parts)]
        masks = [None] * n_parts
        if with_new_tokens:
            pad = 128 - t_new
            lb = jnp.concatenate([cn_ref[...], jnp.zeros((pad, MLA_KV_RANK), F32)], 0).astype(BF16)
            kb = jnp.concatenate([krn_ref[...], jnp.zeros((pad, MLA_ROPE), F32)], 0).T.astype(BF16)
            ib = jnp.concatenate([irn_ref[...], jnp.zeros((pad, 128), F32)], 0).T[:MLA_HEADS]
            tq = lax.rem(lax.broadcasted_iota(jnp.int32, (rows, 128), 0), t_new)
            key = lax.broadcasted_iota(jnp.int32, (rows, 128), 1)
            parts.append((lb, kb, ib))
            masks.append(key <= tq)
        m_new, l, acc = attend(parts, masks)
        if with_new_tokens:
            o_lat = acc / l
            for h in range(MLA_HEADS):
                o_ref[:, h * MLA_V:(h + 1) * MLA_V] = _dot(o_lat[h * t_new:(h + 1) * t_new].astype(BF16), wuv_ref[h])
        else:
            m_sc[...] = m_new
            l_sc[...] = l
            acc_sc[...] = acc

    pl.when(c < n_chunks - 1)(lambda: step(False))
    pl.when(c == n_chunks - 1)(lambda: step(True))

    @pl.when(t == last)
    def _():
        for ahead in range(1, DECODE_SLOTS):
            for cp in chunk_copies(last, lax.rem(t + ahead, DECODE_SLOTS)):
                cp.wait()


def _decode(page_table, ql, qr, c_new, kr_new, ir_new, wuv_h, lat_pool, krt_pool, irt_pool, pages):
    bs, n_pages = page_table.shape
    t_new = c_new.shape[1]
    page = lat_pool.shape[1]
    n_chunks = n_pages // pages
    keys = pages * page
    rows = MLA_HEADS * t_new
    per_b = lambda shape: pl.BlockSpec((None,) + shape, lambda t, pt: (t // n_chunks,) + (0,) * len(shape))
    hbm = pl.BlockSpec(memory_space=pl.ANY)
    in_specs = [pl.BlockSpec((MLA_HEADS, None, t_new, MLA_KV_RANK), lambda t, pt: (0, t // n_chunks, 0, 0)),
                pl.BlockSpec((MLA_HEADS, None, t_new, MLA_ROPE), lambda t, pt: (0, t // n_chunks, 0, 0)),
                per_b((t_new, MLA_KV_RANK)), per_b((t_new, MLA_ROPE)), per_b((t_new, 128)),
                pl.BlockSpec(wuv_h.shape, lambda t, pt: (0, 0, 0)), hbm, hbm, hbm]
    grid_spec = pltpu.PrefetchScalarGridSpec(
        num_scalar_prefetch=1,
        grid=(bs * n_chunks,),
        in_specs=in_specs,
        out_specs=per_b((t_new, MLA_HEADS * MLA_V)),
        scratch_shapes=[pltpu.VMEM((rows, 1), F32), pltpu.VMEM((rows, 1), F32), pltpu.VMEM((rows, MLA_KV_RANK), F32),
                        pltpu.VMEM((DECODE_SLOTS, keys, MLA_KV_RANK), F32),
                        pltpu.VMEM((DECODE_SLOTS, pages, MLA_ROPE, page), F32),
                        pltpu.VMEM((DECODE_SLOTS, pages, MLA_HEADS, page), F32),
                        pltpu.SemaphoreType.DMA((DECODE_SLOTS, 3))],
    )
    return pl.pallas_call(
        functools.partial(_decode_body, pages=pages, n_chunks=n_chunks, t_new=t_new),
        grid_spec=grid_spec,
        out_shape=jax.ShapeDtypeStruct((bs, t_new, MLA_HEADS * MLA_V), F32),
        compiler_params=_cparams("arbitrary"),
        name="mla_sample_attn",
    )(page_table.reshape(-1), ql, qr, c_new, kr_new, ir_new, wuv_h, lat_pool, krt_pool, irt_pool)


def _gla_body(gq_ref, gk_ref, gv_ref, r_ref, alr_ref, wga_ref, bga_ref, gn_ref, *rest, chunk, sub, nb, carry):
    if carry:
        y_ref, s_out_ref = rest
        s_in_ref = s_out_ref

        @pl.when(pl.program_id(0) == 0)
        def _():
            s_out_ref[...] = jnp.zeros(s_out_ref.shape, F32)
    else:
        s_in_ref, y_ref, s_out_ref = rest
    row = lax.broadcasted_iota(jnp.int32, (chunk, chunk), 0)
    col = lax.broadcasted_iota(jnp.int32, (chunk, chunk), 1)
    tri = (col <= row).astype(F32)
    ones_cv = jnp.ones((chunk, GLA_DV), F32)
    brow = lax.broadcasted_iota(jnp.int32, (sub, chunk), 0)
    bcol = lax.broadcasted_iota(jnp.int32, (sub, chunk), 1)
    hi = lax.Precision.HIGHEST
    gn = gn_ref[...]

    def group(bis):
        chains = [(n, h) for n in range(len(bis)) for h in range(GLA_HEADS)]
        ks = lambda h: slice(h * GLA_DK, (h + 1) * GLA_DK)
        vs = lambda h: slice(h * GLA_DV, (h + 1) * GLA_DV)
        gate = []
        for bi in bis:
            a = _dot(alr_ref[bi].astype(BF16), wga_ref[...]) + bga_ref[...]
            gate.append((jnp.minimum(a, 0.0) - jnp.log1p(jnp.exp(-jnp.abs(a)))) / GLA_TAU)
        cum = [_dot(tri, g, precision=hi) for g in gate]
        q = {(n, h): gq_ref[bis[n], :, ks(h)] * GLA_DK ** -0.5 for n, h in chains}
        k = {(n, h): gk_ref[bis[n], :, ks(h)] for n, h in chains}
        vb = {(n, h): gv_ref[bis[n], :, vs(h)].astype(BF16) for n, h in chains}
        s0 = {(n, h): s_in_ref[bis[n], h] for n, h in chains}
        bc = {(n, h): cum[n][:, ks(h)] for n, h in chains}
        o = {ch: _dot((q[ch] * jnp.exp(bc[ch])).astype(BF16), s0[ch].astype(BF16)) for ch in chains}
        for n, h in chains:
            ch = (n, h)
            dec = jnp.exp(_dot_tn(gate[n][:, ks(h)], ones_cv, precision=hi))
            kd = (k[ch] * jnp.exp(bc[ch][chunk - 1:chunk] - bc[ch])).astype(BF16)
            s_out_ref[bis[n], h] = dec * s0[ch] + _dot_tn(kd, vb[ch])
        for blk in range(chunk // sub):
            r0 = blk * sub
            a_blk = {}
            for ch in chains:
                qb = q[ch][r0:r0 + sub]
                bb = bc[ch][r0:r0 + sub]
                ab = jnp.zeros((sub, chunk), F32)
                for s in range(sub):
                    e = jnp.exp(jnp.minimum(bb - bb[s:s + 1], 0.0))
                    cs = jnp.sum(qb * e * k[ch][r0 + s:r0 + s + 1], -1, keepdims=True)
                    ab = jnp.where(bcol == r0 + s, jnp.where(brow >= s, cs, 0.0), ab)
                if blk > 0:
                    b0 = bc[ch][r0 - 1:r0]
                    qs = (qb * jnp.exp(bb - b0)).astype(BF16)
                    ksc = (k[ch] * jnp.exp(jnp.minimum(b0 - bc[ch], 0.0))).astype(BF16)
                    ab = jnp.where(bcol < r0, _dot_nt(qs, ksc), ab)
                a_blk[ch] = ab
            for n, h in chains:
                ch = (n, h)
                ob = o[ch][r0:r0 + sub] + _dot(a_blk[ch].astype(BF16), vb[ch])
                rr = r_ref[bis[n], r0:r0 + sub, vs(h)]
                y_ref[bis[n], r0:r0 + sub, vs(h)] = _rms(ob, gn) * (rr * jax.nn.sigmoid(rr))

    if carry:
        group(list(range(nb)))
    else:
        def rows(i, carry_):
            group([i * GLA_ROWS_PER_ITER + j for j in range(GLA_ROWS_PER_ITER)])
            return carry_

        lax.fori_loop(0, nb // GLA_ROWS_PER_ITER, rows, 0)


def _gla(z3, state, wga, bga, gn, chunk, sub, nb):
    b, s, _ = z3.shape
    carry = state is None
    if carry:
        grid = (s // chunk,)
        zmap = lambda blk: (lambda c: (0, c, blk))
        smap = lambda c: (0, 0, 0, 0)
        nb = b
    else:
        grid = (b // nb,)
        zmap = lambda blk: (lambda i: (i, 0, blk))
        smap = lambda i: (i, 0, 0, 0)
    zspec = lambda wb: pl.BlockSpec((nb, chunk, wb[0]), zmap(wb[1]))
    const = lambda shape: pl.BlockSpec(shape, lambda i: (0,) * len(shape))
    sspec = pl.BlockSpec((nb, GLA_HEADS, GLA_DK, GLA_DV), smap)
    in_specs = [zspec(ZB_GQ), zspec(ZB_GK), zspec(ZB_GV), zspec(ZB_R), zspec(ZB_ALR),
                const(wga.shape), const(bga.shape), const(gn.shape)]
    args = [z3, z3, z3, z3, z3, wga, bga, gn]
    if not carry:
        in_specs.append(sspec)
        args.append(state)
    return pl.pallas_call(
        functools.partial(_gla_body, chunk=chunk, sub=sub, nb=nb, carry=carry),
        grid=grid,
        in_specs=in_specs,
        out_specs=[pl.BlockSpec((nb, chunk, GLA_HEADS * GLA_DV), zmap(0)), sspec],
        out_shape=[jax.ShapeDtypeStruct((b, s, GLA_HEADS * GLA_DV), F32),
                   jax.ShapeDtypeStruct((b, GLA_HEADS, GLA_DK, GLA_DV), F32)],
        compiler_params=_cparams("arbitrary"),
        name="gla_prompt" if carry else "gla_sample",
    )(*args)


def _out_proj_body(mla_ref, gla_ref, x_ref, nm_ref, w_ref, o_ref, *, mla_feature_major):
    mla = mla_ref[...].T if mla_feature_major else mla_ref[...]
    half = mla.shape[1]
    mn = _rms(mla, nm_ref[...]).astype(BF16)
    o_ref[...] = x_ref[...] + _dot(mn, w_ref[:half]) + _dot(gla_ref[...].astype(BF16), w_ref[half:])


def _out_proj(mla_o, gla_y, x, nm, w, tm):
    n, d = x.shape
    half = gla_y.shape[1]
    feature_major = mla_o.ndim == 3
    if feature_major:
        per_b = mla_o.shape[2] // tm
        mla_spec = pl.BlockSpec((None, half, tm), lambda i: (i // per_b, 0, lax.rem(i, per_b)))
    else:
        mla_spec = pl.BlockSpec((tm, half), lambda i: (i, 0))
    return pl.pallas_call(
        functools.partial(_out_proj_body, mla_feature_major=feature_major),
        grid=(n // tm,),
        in_specs=[mla_spec, pl.BlockSpec((tm, half), lambda i: (i, 0)),
                  pl.BlockSpec((tm, d), lambda i: (i, 0)), pl.BlockSpec((1, half), lambda i: (0, 0)),
                  pl.BlockSpec(w.shape, lambda i: (0, 0))],
        out_specs=pl.BlockSpec((tm, d), lambda i: (i, 0)),
        out_shape=jax.ShapeDtypeStruct((n, d), F32),
        compiler_params=_cparams("parallel"),
        name="out_proj",
    )(mla_o, gla_y, x, nm.reshape(1, half), w)


def _top_desc(s, n):
    vals = []
    w = s
    for _ in range(n):
        m = jnp.max(w, axis=0, keepdims=True)
        vals.append(m)
        w = jnp.where(w == m, NEG_INF, w)
    return vals


def _peer_route_body(h_ref, g_ref, wqt_ref, k1_ref, k2_ref, xnt_ref, s2_ref, e2_ref, thr_ref, e1z_ref):
    xnt = _rms(h_ref[...], g_ref[...]).T.astype(BF16)
    xnt_ref[...] = xnt
    qt = _dot(wqt_ref[...], xnt).astype(BF16)
    half = PEER_N_KEYS
    for h in range(PEER_HEADS):
        s1 = _dot(k1_ref[...], qt[(2 * h) * half:(2 * h + 1) * half])
        s2 = _dot(k2_ref[...], qt[(2 * h + 1) * half:(2 * h + 2) * half])
        v1 = _top_desc(s1, PEER_TOPK)
        v2 = _top_desc(s2, PEER_TOPK)
        pairs = [(a, b) for a in range(PEER_TOPK) for b in range(PEER_TOPK // (a + 1))]
        sums = {ab: v1[ab[0]] + v2[ab[1]] for ab in pairs}
        npad = -len(pairs) % 8
        stack = jnp.concatenate([sums[ab] for ab in pairs] + [jnp.full_like(v1[0], NEG_INF)] * npad, 0)
        tau = _top_desc(stack, PEER_TOPK)[-1]
        top = v1[0] + v2[0]
        zsum = jnp.sum(jnp.where(stack >= tau, jnp.exp(stack - top), 0.0), axis=0, keepdims=True)
        thr = jnp.full(s1.shape, POS_INF, F32)
        for a in range(PEER_TOPK):
            ta = jnp.full_like(tau, POS_INF)
            for b in range(PEER_TOPK // (a + 1)):
                ta = jnp.minimum(ta, jnp.where(sums[(a, b)] >= tau, v2[b], POS_INF))
            thr = jnp.where(s1 == v1[a], ta, thr)
        e2 = jnp.exp(s2 - v2[0])
        e1z = jnp.exp(s1 - v1[0]) / zsum
        for lg in range(thr.shape[1] // 128):
            lanes = slice(lg * 128, (lg + 1) * 128)
            s2_ref[h, lg] = s2[:, lanes]
            e2_ref[h, lg] = e2[:, lanes]
            thr_ref[h, lg] = thr[:, lanes]
            e1z_ref[h, lg] = e1z[:, lanes]


def _peer_route(h, g, wqt, k1, k2, tt):
    n, d = h.shape
    per_group = jax.ShapeDtypeStruct((PEER_HEADS, n // 128, PEER_N_KEYS, 128), F32)
    gspec = pl.BlockSpec((PEER_HEADS, tt // 128, PEER_N_KEYS, 128), lambda i: (0, i, 0, 0))
    return pl.pallas_call(
        _peer_route_body,
        grid=(n // tt,),
        in_specs=[pl.BlockSpec((tt, d), lambda i: (i, 0)), pl.BlockSpec((1, d), lambda i: (0, 0)),
                  pl.BlockSpec(wqt.shape, lambda i: (0, 0)), pl.BlockSpec(k1.shape, lambda i: (0, 0)),
                  pl.BlockSpec(k2.shape, lambda i: (0, 0))],
        out_specs=[pl.BlockSpec((d, tt), lambda i: (0, i)), gspec, gspec, gspec, gspec],
        out_shape=[jax.ShapeDtypeStruct((d, n), BF16), per_group, per_group, per_group, per_group],
        compiler_params=_cparams("parallel"),
        name="peer_route",
    )(h, g.reshape(1, d), wqt, k1, k2)


def _peer_dense_body(xnt_ref, u_ref, vt_ref, s2_ref, e2_ref, thr_ref, e1z_ref, yt_ref, act_sc, ca_sc):
    @pl.when(pl.program_id(1) == 0)
    def _():
        yt_ref[...] = jnp.zeros(yt_ref.shape, F32)

    n_lg, te, _ = act_sc.shape
    act = _dot(u_ref[...], xnt_ref[...])
    act = 0.5 * act * (1.0 + lax.erf(act * (2.0 ** -0.5)))
    for lg in range(n_lg):
        act_sc[lg] = act[:, lg * 128:(lg + 1) * 128]
    groups = te // PEER_N_KEYS
    n_sub = PEER_N_KEYS // PEER_SUB

    def gate_block(blk, carry):
        lg = blk // n_sub
        k0 = pl.multiple_of(lax.rem(blk, n_sub) * PEER_SUB, PEER_SUB)
        coef = [jnp.zeros((PEER_SUB, 128), F32)] * groups
        for h in range(PEER_HEADS):
            s2b = s2_ref[h, lg, pl.ds(k0, PEER_SUB), :]
            e2b = e2_ref[h, lg, pl.ds(k0, PEER_SUB), :]
            for gi in range(groups):
                thr = jnp.broadcast_to(thr_ref[h, lg, gi:gi + 1, :], (PEER_SUB, 128))
                e1z = jnp.broadcast_to(e1z_ref[h, lg, gi:gi + 1, :], (PEER_SUB, 128))
                coef[gi] = coef[gi] + jnp.where(s2b >= thr, e2b, 0.0) * e1z
        for gi in range(groups):
            rows = pl.ds(pl.multiple_of(gi * PEER_N_KEYS + k0, PEER_SUB), PEER_SUB)
            ca_sc[lg, rows, :] = (coef[gi] * act_sc[lg, rows, :]).astype(BF16)
        return carry

    lax.fori_loop(0, n_lg * n_sub, gate_block, 0)
    ca = jnp.concatenate([ca_sc[lg] for lg in range(n_lg)], 1)
    yt_ref[...] += _dot(vt_ref[...], ca)


def _peer_dense(xnt, u, vt, s2, e2, thr, e1z, tt, te):
    d, n = xnt.shape
    ne = u.shape[0] // te
    gpt = te // PEER_N_KEYS
    full = pl.BlockSpec((PEER_HEADS, tt // 128, PEER_N_KEYS, 128), lambda i, j: (0, i, 0, 0))
    part = pl.BlockSpec((PEER_HEADS, tt // 128, gpt, 128), lambda i, j: (0, i, j, 0))
    return pl.pallas_call(
        _peer_dense_body,
        grid=(n // tt, ne),
        in_specs=[pl.BlockSpec((d, tt), lambda i, j: (0, i)), pl.BlockSpec((te, d), lambda i, j: (j, 0)),
                  pl.BlockSpec((d, te), lambda i, j: (0, j)), full, full, part, part],
        out_specs=pl.BlockSpec((d, tt), lambda i, j: (0, i)),
        out_shape=jax.ShapeDtypeStruct((d, n), F32),
        scratch_shapes=[pltpu.VMEM((tt // 128, te, 128), F32), pltpu.VMEM((tt // 128, te, 128), BF16)],
        compiler_params=_cparams("parallel", "arbitrary"),
        name="peer_dense",
    )(xnt, u, vt, s2, e2, thr, e1z)


def _ple_body(h_ref, yt_ref, p_ref, g_ref, wg_ref, wp_ref, o_ref):
    h2 = h_ref[...] + yt_ref[...].T
    gate = jax.nn.sigmoid(_dot(_rms(h2, g_ref[...]).astype(BF16), wg_ref[...]))
    o_ref[...] = h2 + gate * _dot(p_ref[...].astype(BF16), wp_ref[...])


def _ple(h, yt, p, g, wg, wp, tm):
    n, d = h.shape
    return pl.pallas_call(
        _ple_body,
        grid=(n // tm,),
        in_specs=[pl.BlockSpec((tm, d), lambda i: (i, 0)), pl.BlockSpec((d, tm), lambda i: (0, i)),
                  pl.BlockSpec((tm, p.shape[1]), lambda i: (i, 0)), pl.BlockSpec((1, d), lambda i: (0, 0)),
                  pl.BlockSpec(wg.shape, lambda i: (0, 0)), pl.BlockSpec(wp.shape, lambda i: (0, 0))],
        out_specs=pl.BlockSpec((tm, d), lambda i: (i, 0)),
        out_shape=jax.ShapeDtypeStruct((n, d), F32),
        compiler_params=_cparams("parallel"),
        name="ple",
    )(h, yt, p, g.reshape(1, d), wg, wp)


def _rope_table(pos):
    half = MLA_ROPE // 2
    inv_freq = ROPE_THETA ** (-jnp.arange(half, dtype=F32) / half)
    ang = pos.astype(F32)[:, None] * inv_freq
    cos, sin = jnp.cos(ang), jnp.sin(ang)
    return jnp.concatenate([cos, cos, sin, sin], -1)


def _tile(n, t):
    return t if n % t == 0 else n


def _finish(x2, mla_o, gla_y, p2, w):
    n = x2.shape[0]
    h = _out_proj(mla_o, gla_y, x2, w["norm_mla_out"], w["w_out"], _tile(n, TM_OUT))
    xnt, s2, e2, thr, e1z = _peer_route(h, w["norm_ffn"], w["wqt"], w["k1"], w["k2"], _tile(n, TT_ROUTE))
    yt = _peer_dense(xnt, w["u"], w["vt"], s2, e2, thr, e1z, _tile(n, TT_DENSE), TE_DENSE)
    return _ple(h, yt, p2, w["norm_ple"], w["w_ple_gate"], w["w_ple_proj"], _tile(n, TM_PLE))


def kernel(x_prompt, x_sample, cache_kv_latent, cache_k_rope, cache_k_inv_rms, state_gla, page_table, p_prompt,
           p_sample, norm_mix, w_in, norm_q_lat, w_uq, norm_kv_lat, w_uk, w_uv, qk_gain_q, qk_gain_k, norm_mla_out,
           w_gla_a, b_gla_a, norm_gla_out, w_out, norm_ffn, peer_w_q, peer_keys1, peer_keys2, peer_u, peer_v,
           norm_ple, w_ple_gate, w_ple_proj):
    depth = w_in.shape[0]
    b, s, d = x_prompt.shape
    bs, t_new, _ = x_sample.shape
    n_pages = page_table.shape[1]
    page = cache_kv_latent.shape[2]
    past_len = n_pages * page
    tab_p = _rope_table(jnp.arange(s))
    tab_s = _rope_table(past_len + jnp.arange(t_new))
    hp, hs = x_prompt.reshape(b * s, d), x_sample.reshape(bs * t_new, d)
    outs = [[] for _ in range(8)]
    for i in range(depth):
        wi = w_in[i]
        cuts = [0, 512, 768, 832, 1344, 1856, 2880, 3904, 3920]
        cq_w, ckv_w, kr_w, gq_w, gk_w, gv_w, r_w, alr_w = [wi[:, cuts[j]:cuts[j + 1]] for j in range(8)]
        w_in_ext = jnp.concatenate(
            [cq_w, gq_w, gk_w, ckv_w, kr_w, _rot_cols(kr_w), alr_w,
             jnp.zeros((d, 128 - GLA_GATE_RANK), F32), gv_w, r_w], 1).astype(BF16)
        wuq3 = w_uq[i].reshape(MLA_Q_RANK, MLA_HEADS, MLA_QK)
        wuq_ext = jnp.concatenate([wuq3, _rot_cols(wuq3[..., MLA_NOPE:])], -1).reshape(MLA_Q_RANK, -1).astype(BF16)
        gq, gk = qk_gain_q[i], qk_gain_k[i]
        gq_ext = jnp.concatenate([gq, _rot_gain(gq[MLA_NOPE:])]).reshape(1, -1)
        gk_ext = jnp.concatenate([gk, _rot_gain(gk[MLA_NOPE:])]).reshape(1, -1)
        wuk = w_uk[i].reshape(MLA_KV_RANK, -1).astype(BF16)
        wuv = w_uv[i].reshape(MLA_KV_RANK, -1).astype(BF16)
        wuk_t = jnp.transpose(w_uk[i], (1, 2, 0)).astype(BF16)
        wuv_h = jnp.transpose(w_uv[i], (1, 0, 2)).astype(BF16)
        wga = jnp.concatenate([w_gla_a[i], jnp.zeros((128 - GLA_GATE_RANK, w_gla_a.shape[2]), F32)], 0).astype(BF16)
        bga = b_gla_a[i].reshape(1, -1)
        gn = norm_gla_out[i].reshape(1, -1)
        w = {"norm_mla_out": norm_mla_out[i], "w_out": w_out[i].astype(BF16), "norm_ffn": norm_ffn[i],
             "wqt": peer_w_q[i].T.astype(BF16), "k1": peer_keys1[i].astype(BF16), "k2": peer_keys2[i].astype(BF16),
             "u": peer_u[i].astype(BF16), "vt": peer_v[i].T.astype(BF16), "norm_ple": norm_ple[i],
             "w_ple_gate": w_ple_gate[i].astype(BF16), "w_ple_proj": w_ple_proj[i].astype(BF16)}
        prep = (norm_q_lat[i], norm_kv_lat[i], gq_ext, gk_ext, wuq_ext, wuk)

        n = b * s
        z = _norm_matmul(hp, norm_mix[i], w_in_ext, _tile(n, TM_IN), TN_IN)
        tm = _tile(s, TM_PREP)
        qt, kt, vt, lat, kr, ir = _mla_prep(z, tab_p, s // tm, *prep, wuv, tm, sample=False)
        tq = _tile(s, FLASH_TILE)
        mla_o = _flash(qt.reshape(b, s, -1), kt.reshape(b, s, -1), vt, tq, FLASH_HEADS)
        gla_y, s_fin = _gla(z.reshape(b, s, Z_COLS), None, wga, bga, gn, GLA_CHUNK, GLA_SUB, b)
        hp_next = _finish(hp, mla_o, gla_y.reshape(n, -1), p_prompt[i].reshape(n, -1), w)
        outs[0].append(lat.reshape(b, s, -1))
        outs[1].append(kr.reshape(b, s, -1))
        outs[2].append(ir[:, :MLA_HEADS].reshape(b, s, -1))
        outs[3].append(s_fin)

        n = bs * t_new
        z = _norm_matmul(hs, norm_mix[i], w_in_ext, _tile(n, TM_IN), TN_IN)
        tm = _tile(n, TM_PREP)
        tab = jnp.tile(tab_s, (tm // t_new, 1))
        ql, qr, lat, kr, ir = _mla_prep(z, tab, 1, *prep, wuk_t, tm, sample=True)
        mla_o = _decode(page_table, ql.reshape(MLA_HEADS, bs, t_new, -1), qr.reshape(MLA_HEADS, bs, t_new, -1),
                        lat.reshape(bs, t_new, -1), kr.reshape(bs, t_new, -1), ir.reshape(bs, t_new, -1), wuv_h,
                        cache_kv_latent[i], jnp.transpose(cache_k_rope[i], (0, 2, 1)),
                        jnp.transpose(cache_k_inv_rms[i], (0, 2, 1)), _tile(n_pages, DECODE_PAGES))
        gla_y, s_new = _gla(z.reshape(bs, t_new, Z_COLS), state_gla[i], wga, bga, gn, t_new, t_new,
                            _tile(bs, GLA_SAMPLE_ROWS))
        hs_next = _finish(hs, mla_o.reshape(n, -1), gla_y.reshape(n, -1), p_sample[i].reshape(n, -1), w)
        outs[4].append(lat.reshape(bs, t_new, -1))
        outs[5].append(kr.reshape(bs, t_new, -1))
        outs[6].append(ir[:, :MLA_HEADS].reshape(bs, t_new, -1))
        outs[7].append(s_new)
        hp, hs = hp_next, hs_next
    return (hp.reshape(b, s, d), hs.reshape(bs, t_new, d)) + tuple(jnp.stack(o, 0) for o in outs)
```
